```python
import jax, jax.numpy as jnp
from jax import lax
import numpy as np

D_MODEL = 2048
BATCH = 32
SEQ = 256
DEPTH = 1
DEC_BATCH = 4
DEC_SEQ = 4096
PAST_LEN = 256

GRID_W = 64
HEAD_DIM = 128
ATT_HEADS = 8
ATT_KV_HEADS = 2
ATT_WIDTH = ATT_HEADS * HEAD_DIM
KV_WIDTH = ATT_KV_HEADS * HEAD_DIM
HG_HEADS = 8
HG_DK = 128
HG_DV = 128
HG_WIDTH = HG_HEADS * HG_DK
HG_VWIDTH = HG_HEADS * HG_DV
D_FF = -(-8 * D_MODEL // 768) * 256
ROPE_AXIS_DIM = HEAD_DIM // 2
ROPE_THETA = 10000.0
Q_BLOCK = 128
HG_CHUNK = 32
NORM_EPS = 1e-6
IN_WIDTH = ATT_WIDTH + 2 * KV_WIDTH + 3 * HG_WIDTH + 2 * HG_VWIDTH + 2 * D_MODEL

kernel_name = "hybrid_gqa_hgrn2_diffusion_step"


def rmsnorm(x, g):
    xf = x.astype(jnp.float32)
    y = xf * lax.rsqrt(jnp.mean(xf * xf, axis=-1, keepdims=True) + NORM_EPS)
    return (y * g.astype(jnp.float32)).astype(x.dtype)


def modulated_norm(x, g, shift, scale):
    return rmsnorm(x, g) * (1 + scale) + shift


def ada_mod(cond, w, b):
    m = jax.nn.silu(cond) @ w + b
    return jnp.split(m[:, None, :], 6, axis=-1)


def split_projection(h, w):
    sizes = (ATT_WIDTH, KV_WIDTH, KV_WIDTH, HG_WIDTH, HG_WIDTH, HG_WIDTH, HG_VWIDTH, HG_VWIDTH, D_MODEL, D_MODEL)
    offsets = np.cumsum(sizes)[:-1].tolist()
    return jnp.split(h @ w, offsets, axis=-1)


def axial_rope_tables(n_tokens):
    rows = n_tokens // GRID_W
    half = ROPE_AXIS_DIM // 2
    r = jnp.repeat(jnp.arange(rows), GRID_W).astype(jnp.float32)
    col = jnp.tile(jnp.arange(GRID_W), rows).astype(jnp.float32)
    inv = ROPE_THETA ** (-jnp.arange(half, dtype=jnp.float32) / half)
    ang = jnp.concatenate([r[:, None] * inv, col[:, None] * inv], axis=-1)
    return jnp.cos(ang), jnp.sin(ang)


def apply_axial_rope(x, cos, sin):
    half = ROPE_AXIS_DIM // 2
    xf = x.astype(jnp.float32)

    def rot(a, cs, sn):
        a1, a2 = a[..., :half], a[..., half:]
        cs = cs[None, :, None, :]
        sn = sn[None, :, None, :]
        return jnp.concatenate([a1 * cs - a2 * sn, a2 * cs + a1 * sn], axis=-1)

    xr = rot(xf[..., :ROPE_AXIS_DIM], cos[:, :half], sin[:, :half])
    xc = rot(xf[..., ROPE_AXIS_DIM:], cos[:, half:], sin[:, half:])
    return jnp.concatenate([xr, xc], axis=-1).astype(x.dtype)


def attention_qkv(qa, ka, va, q_norm, k_norm):
    B, T, _ = qa.shape
    q = rmsnorm(qa.reshape(B, T, ATT_HEADS, HEAD_DIM), q_norm)
    k = rmsnorm(ka.reshape(B, T, ATT_KV_HEADS, HEAD_DIM), k_norm)
    v = va.reshape(B, T, ATT_KV_HEADS, HEAD_DIM)
    return q, k, v


def block_attention(q, k, v):
    B, T, H, hd = q.shape
    G = H // ATT_KV_HEADS
    nb = T // Q_BLOCK
    qb = q.reshape(B, nb, Q_BLOCK, ATT_KV_HEADS, G, hd).transpose(1, 0, 2, 3, 4, 5).astype(jnp.float32)
    kf = k.astype(jnp.float32)
    vf = v.astype(jnp.float32)
    scale = HEAD_DIM ** -0.5

    def one_block(qblk):
        s = jnp.einsum("bqkgd,bskd->bkgqs", qblk, kf) * scale
        p = jax.nn.softmax(s, axis=-1)
        return jnp.einsum("bkgqs,bskd->bqkgd", p, vf)

    o = lax.map(one_block, qb)
    return o.transpose(1, 0, 2, 3, 4, 5).reshape(B, T, H * hd).astype(q.dtype)


def lower_bound(lb_param, l):
    return jnp.cumsum(jax.nn.softmax(lb_param.astype(jnp.float32), axis=0), axis=0)[l]


def forget_gate(z, lb):
    zf = z.astype(jnp.float32)
    logf = jnp.logaddexp(jnp.log(lb), jnp.log1p(-lb) + jax.nn.log_sigmoid(zf))
    k = (1 - lb) * jax.nn.sigmoid(-zf)
    return logf, k


def hgrn2_chunk_scan(q, k, logf, v, s0):
    B, T, H, dk = q.shape
    dv = v.shape[-1]
    n = T // HG_CHUNK

    def to_chunks(a):
        return a.reshape(B, n, HG_CHUNK, H, a.shape[-1]).transpose(1, 0, 3, 2, 4)

    mask = jnp.tril(jnp.ones((HG_CHUNK, HG_CHUNK), dtype=bool))[:, :, None]

    def step(S, inp):
        qc, kc, lfc, vc = inp
        bcum = jnp.cumsum(lfc, axis=2)
        diff = bcum[:, :, :, None, :] - bcum[:, :, None, :, :]
        decay = jnp.where(mask, jnp.exp(jnp.where(mask, diff, 0.0)), 0.0)
        a = jnp.einsum("bhtd,bhsd,bhtsd->bhts", qc, kc, decay)
        o = jnp.einsum("bhts,bhsv->bhtv", a, vc) + jnp.einsum("bhtd,bhdv->bhtv", qc * jnp.exp(bcum), S)
        btot = bcum[:, :, -1:, :]
        S_new = jnp.exp(btot[:, :, 0, :])[..., None] * S + jnp.einsum("bhsd,bhsv->bhdv", kc * jnp.exp(btot - bcum), vc)
        return S_new, o

    s_fin, o = lax.scan(step, s0.astype(jnp.float32), (to_chunks(q), to_chunks(k), to_chunks(logf), to_chunks(v)))
    o = o.transpose(1, 0, 3, 2, 4).reshape(B, T, H, dv)
    return o, s_fin


def hgrn2_branch(q_raw, zf_raw, zb_raw, v_raw, og_raw, lb_f, lb_b, hg_norm, s_f0, s_b0):
    B, T, _ = q_raw.shape
    q = jax.nn.silu(q_raw.astype(jnp.float32)).reshape(B, T, HG_HEADS, HG_DK)
    v = v_raw.astype(jnp.float32).reshape(B, T, HG_HEADS, HG_DV)
    logf_f, k_f = forget_gate(zf_raw, lb_f)
    logf_b, k_b = forget_gate(zb_raw, lb_b)
    shp = (B, T, HG_HEADS, HG_DK)
    o_f, s_f = hgrn2_chunk_scan(q, k_f.reshape(shp), logf_f.reshape(shp), v, s_f0)

    def flip(a):
        return jnp.flip(a, axis=1)

    o_b, s_b = hgrn2_chunk_scan(flip(q), flip(k_b.reshape(shp)), flip(logf_b.reshape(shp)), flip(v), s_b0)
    o = o_f + flip(o_b)
    o = rmsnorm(o, hg_norm) * jax.nn.silu(og_raw.astype(jnp.float32)).reshape(B, T, HG_HEADS, HG_DV)
    return o.reshape(B, T, HG_VWIDTH).astype(q_raw.dtype), s_f, s_b


def merge_branches(att, hg, ga, gb, w_br_att, w_br_hg, w_out):
    m = jax.nn.sigmoid(ga) * (att @ w_br_att) + jax.nn.sigmoid(gb) * (hg @ w_br_hg)
    return m @ w_out


def ffn_sublayer(x, sh2, sc2, g2, lp):
    h = modulated_norm(x, lp["norm_pre_ffn"], sh2, sc2)
    gate, up = jnp.split(h @ lp["w_ffn_in"], 2, axis=-1)
    f = (jax.nn.silu(gate) * up) @ lp["w_ffn_out"]
    return x + g2 * rmsnorm(f, lp["norm_post_ffn"])


def context_layer(x, c_ctx, lp, lb_f, lb_b):
    B, T, _ = x.shape
    sh1, sc1, g1, sh2, sc2, g2 = ada_mod(c_ctx[None, :], lp["w_ada"], lp["b_ada"])
    h = modulated_norm(x, lp["norm_pre_mix"], sh1, sc1)
    qa, ka, va, qh, zf, zb, vh, og, ga, gb = split_projection(h, lp["w_in"])
    q, k, v = attention_qkv(qa, ka, va, lp["q_norm"], lp["k_norm"])
    att = block_attention(q, k, v)
    zero = jnp.zeros((B, HG_HEADS, HG_DK, HG_DV), jnp.float32)
    hg, s_f, s_b = hgrn2_branch(qh, zf, zb, vh, og, lb_f, lb_b, lp["hg_norm"], zero, zero)
    mo = merge_branches(att, hg, ga, gb, lp["w_br_att"], lp["w_br_hg"], lp["w_out"])
    x = x + g1 * rmsnorm(mo, lp["norm_post_mix"])
    x = ffn_sublayer(x, sh2, sc2, g2, lp)
    return x, k, v, s_f, s_b


def latent_layer(x, c, k_ctx, v_ctx, s_f0, s_b0, lp, lb_f, lb_b, cos, sin):
    sh1, sc1, g1, sh2, sc2, g2 = ada_mod(c, lp["w_ada"], lp["b_ada"])
    h = modulated_norm(x, lp["norm_pre_mix"], sh1, sc1)
    qa, ka, va, qh, zf, zb, vh, og, ga, gb = split_projection(h, lp["w_in"])
    q, k, v = attention_qkv(qa, ka, va, lp["q_norm"], lp["k_norm"])
    q = apply_axial_rope(q, cos, sin)
    k = apply_axial_rope(k, cos, sin)
    k_all = jnp.concatenate([k, k_ctx.astype(k.dtype)], axis=1)
    v_all = jnp.concatenate([v, v_ctx.astype(v.dtype)], axis=1)
    att = block_attention(q, k_all, v_all)
    hg, _, _ = hgrn2_branch(qh, zf, zb, vh, og, lb_f, lb_b, lp["hg_norm"], s_f0, s_b0)
    mo = merge_branches(att, hg, ga, gb, lp["w_br_att"], lp["w_br_hg"], lp["w_out"])
    x = x + g1 * rmsnorm(mo, lp["norm_post_mix"])
    return ffn_sublayer(x, sh2, sc2, g2, lp)


def setup_inputs(seed: int = 0) -> dict:
    key = jax.random.key(seed)
    ks = jax.random.split(key, 26)

    def nrm(k, shape, scale):
        return jax.random.normal(k, shape, jnp.float32) * scale

    def gain(k, shape):
        return 1.0 + 0.05 * jax.random.normal(k, shape, jnp.float32)

    D = D_MODEL
    return {
        "x_prompt": nrm(ks[0], (BATCH, SEQ, D), 1.0),
        "x_sample": nrm(ks[1], (DEC_BATCH, DEC_SEQ, D), 1.0),
        "cache_k": nrm(ks[2], (DEC_BATCH, DEPTH, PAST_LEN, ATT_KV_HEADS, HEAD_DIM), 1.0),
        "cache_v": nrm(ks[3], (DEC_BATCH, DEPTH, PAST_LEN, ATT_KV_HEADS, HEAD_DIM), 1.0),
        "state_fwd": nrm(ks[4], (DEC_BATCH, DEPTH, HG_HEADS, HG_DK, HG_DV), 0.5),
        "state_bwd": nrm(ks[5], (DEC_BATCH, DEPTH, HG_HEADS, HG_DK, HG_DV), 0.5),
        "c": nrm(ks[6], (DEC_BATCH, D), 1.0),
        "c_ctx": nrm(ks[7], (D,), 1.0),
        "w_ada": nrm(ks[8], (DEPTH, D, 6 * D), D ** -0.5),
        "b_ada": nrm(ks[9], (DEPTH, 6 * D), 0.02),
        "norm_pre_mix": gain(ks[10], (DEPTH, D)),
        "norm_post_mix": gain(ks[11], (DEPTH, D)),
        "norm_pre_ffn": gain(ks[12], (DEPTH, D)),
        "norm_post_ffn": gain(ks[13], (DEPTH, D)),
        "w_in": nrm(ks[14], (DEPTH, D, IN_WIDTH), D ** -0.5),
        "q_norm": gain(ks[15], (DEPTH, HEAD_DIM)),
        "k_norm": gain(ks[16], (DEPTH, HEAD_DIM)),
        "lb_fwd": nrm(ks[17], (DEPTH + 1, HG_WIDTH), 0.5),
        "lb_bwd": nrm(ks[18], (DEPTH + 1, HG_WIDTH), 0.5),
        "hg_norm": gain(ks[19], (DEPTH, HG_DV)),
        "w_br_att": nrm(ks[20], (DEPTH, ATT_WIDTH, D), ATT_WIDTH ** -0.5),
        "w_br_hg": nrm(ks[21], (DEPTH, HG_VWIDTH, D), HG_VWIDTH ** -0.5),
        "w_out": nrm(ks[22], (DEPTH, D, D), D ** -0.5),
        "w_ffn_in": nrm(ks[23], (DEPTH, D, 2 * D_FF), D ** -0.5),
        "w_ffn_out": nrm(ks[24], (DEPTH, D_FF, D), D_FF ** -0.5),
    }


def reference(x_prompt, x_sample, cache_k, cache_v, state_fwd, state_bwd, c, c_ctx, w_ada, b_ada,
              norm_pre_mix, norm_post_mix, norm_pre_ffn, norm_post_ffn, w_in, q_norm, k_norm,
              lb_fwd, lb_bwd, hg_norm, w_br_att, w_br_hg, w_out, w_ffn_in, w_ffn_out):
    cos, sin = axial_rope_tables(x_sample.shape[1])
    xp = x_prompt
    xs = x_sample
    ks_, vs_, sf_, sb_ = [], [], [], []
    for l in range(DEPTH):
        lp = dict(w_ada=w_ada[l], b_ada=b_ada[l], norm_pre_mix=norm_pre_mix[l], norm_post_mix=norm_post_mix[l],
                  norm_pre_ffn=norm_pre_ffn[l], norm_post_ffn=norm_post_ffn[l], w_in=w_in[l],
                  q_norm=q_norm[l], k_norm=k_norm[l], hg_norm=hg_norm[l], w_br_att=w_br_att[l],
                  w_br_hg=w_br_hg[l], w_out=w_out[l], w_ffn_in=w_ffn_in[l], w_ffn_out=w_ffn_out[l])
        lb_f = lower_bound(lb_fwd, l)
        lb_b = lower_bound(lb_bwd, l)
        xp, k_c, v_c, s_f, s_b = context_layer(xp, c_ctx, lp, lb_f, lb_b)
        ks_.append(k_c)
        vs_.append(v_c)
        sf_.append(s_f)
        sb_.append(s_b)
        xs = latent_layer(xs, c, cache_k[:, l], cache_v[:, l], state_fwd[:, l], state_bwd[:, l],
                          lp, lb_f, lb_b, cos, sin)
    new_cache_k = jnp.stack(ks_, axis=1)
    new_cache_v = jnp.stack(vs_, axis=1)
    new_state_fwd = jnp.stack(sf_, axis=1)
    new_state_bwd = jnp.stack(sb_, axis=1)
    return (xp, xs, new_cache_k, new_cache_v, new_state_fwd, new_state_bwd)
```

```python
import functools

import jax
import jax.numpy as jnp
from jax import lax
from jax.experimental import pallas as pl
from jax.experimental.pallas import tpu as pltpu

D_MODEL = 2048
HEAD_DIM = 128
ATT_HEADS = 8
ATT_KV_HEADS = 2
GQA_GROUP = ATT_HEADS // ATT_KV_HEADS
ATT_WIDTH = ATT_HEADS * HEAD_DIM
KV_WIDTH = ATT_KV_HEADS * HEAD_DIM
HG_HEADS = 8
HG_WIDTH = HG_HEADS * HEAD_DIM
D_FF = 5632
GRID_W = 64
ROPE_THETA = 10000.0
NORM_EPS = 1e-6
HG_CHUNK = 32

OFF_Q = 0
OFF_KV = ATT_WIDTH
OFF_QH = OFF_KV + 2 * KV_WIDTH
OFF_ZF = OFF_QH + HG_WIDTH
OFF_ZB = OFF_ZF + HG_WIDTH
OFF_VH = OFF_ZB + HG_WIDTH
OFF_OG = OFF_VH + HG_WIDTH
OFF_GA = OFF_OG + HG_WIDTH
OFF_GB = OFF_GA + D_MODEL

BF = jnp.bfloat16
F32 = jnp.float32

VMEM_LIMIT_BYTES = 56 * 1024 * 1024


def _params(*sem):
    return pltpu.CompilerParams(dimension_semantics=sem, vmem_limit_bytes=VMEM_LIMIT_BYTES)


def _rms(x):
    return x * lax.rsqrt(jnp.mean(x * x, axis=-1, keepdims=True) + NORM_EPS)


def _dot(a, b):
    return jnp.dot(a, b, preferred_element_type=F32)


def _dot_nt(a, b):
    return lax.dot_general(a, b, (((1,), (1,)), ((), ())), preferred_element_type=F32)


def _dot_tn(a, b):
    return lax.dot_general(a, b, (((0,), (0,)), ((), ())), preferred_element_type=F32)


def _ada_kernel(c_ref, w_ref, b_ref, o_ref):
    c = c_ref[...]
    s = (c * jax.nn.sigmoid(c)).astype(BF)
    o_ref[...] = _dot(s, w_ref[...].astype(BF)) + b_ref[...]


def _ada(cond, w, b):
    rows, d = cond.shape
    n = w.shape[1]
    tn = 1536
    return pl.pallas_call(
        _ada_kernel,
        grid=(n // tn,),
        in_specs=[
            pl.BlockSpec((rows, d), lambda j: (0, 0)),
            pl.BlockSpec((d, tn), lambda j: (0, j)),
            pl.BlockSpec((1, tn), lambda j: (0, j)),
        ],
        out_specs=pl.BlockSpec((rows, tn), lambda j: (0, j)),
        out_shape=jax.ShapeDtypeStruct((rows, n), F32),
        compiler_params=_params("arbitrary"),
        name="ada_mod",
    )(cond, w, b)


def _modnorm_kernel(x_ref, g_ref, sc_ref, sh_ref, o_ref):
    y = _rms(x_ref[...]) * g_ref[...]
    o_ref[...] = (y * (1.0 + sc_ref[0]) + sh_ref[0]).astype(BF)


def _modnorm(x, g, sc, sh, seq):
    m, d = x.shape
    tm = 512
    tpb = seq // tm
    return pl.pallas_call(
        _modnorm_kernel,
        grid=(m // tm,),
        in_specs=[
            pl.BlockSpec((tm, d), lambda i: (i, 0)),
            pl.BlockSpec((1, d), lambda i: (0, 0)),
            pl.BlockSpec((1, 1, d), lambda i: (i // tpb, 0, 0)),
            pl.BlockSpec((1, 1, d), lambda i: (i // tpb, 0, 0)),
        ],
        out_specs=pl.BlockSpec((tm, d), lambda i: (i, 0)),
        out_shape=jax.ShapeDtypeStruct((m, d), BF),
        compiler_params=_params("arbitrary"),
        name="modnorm",
    )(x, g, sc, sh)


def _rope(y, cos_ref, sin_ref):
    lane = lax.broadcasted_iota(jnp.int32, y.shape, 1)
    first = (lane & 32) == 0
    swapped = jnp.where(first, pltpu.roll(y, 96, 1), pltpu.roll(y, 32, 1))
    return y * cos_ref[...] + swapped * sin_ref[...]


def _store_heads(o_ref, h, y):
    nbb, _, ts, _ = o_ref.shape
    for bl in range(nbb):
        o_ref[bl, h] = y[bl * ts:(bl + 1) * ts].astype(o_ref.dtype)


def _q_epilogue(rope, acc, refs):
    if rope:
        qn_ref, cos_ref, sin_ref, o_ref = refs
    else:
        qn_ref, o_ref = refs
    scale = HEAD_DIM ** -0.5
    for h in range(ATT_HEADS):
        y = _rms(acc[:, h * HEAD_DIM:(h + 1) * HEAD_DIM]) * qn_ref[...]
        if rope:
            y = _rope(y, cos_ref, sin_ref)
        _store_heads(o_ref, h, y * scale)


def _kv_epilogue(rope, acc, refs):
    if rope:
        kn_ref, cos_ref, sin_ref, kb_ref, vb_ref = refs
    else:
        kn_ref, kb_ref, vb_ref, kf_ref, vf_ref = refs
    for h in range(ATT_KV_HEADS):
        sl = slice(h * HEAD_DIM, (h + 1) * HEAD_DIM)
        k = _rms(acc[:, sl]) * kn_ref[...]
        v = acc[:, KV_WIDTH + h * HEAD_DIM:KV_WIDTH + (h + 1) * HEAD_DIM]
        if rope:
            k = _rope(k, cos_ref, sin_ref)
        else:
            kf_ref[:, sl] = k
            vf_ref[:, sl] = v
        _store_heads(kb_ref, h, k)
        _store_heads(vb_ref, h, v)


def _heads_epilogue(act, acc, refs):
    (o_ref,) = refs
    for h in range(HG_HEADS):
        y = acc[:, h * HEAD_DIM:(h + 1) * HEAD_DIM]
        if act == "silu":
            y = y * jax.nn.sigmoid(y)
        _store_heads(o_ref, h, y)


def _forget_epilogue(acc, refs):
    lbp_ref, lf_ref, k_ref = refs
    p = lbp_ref[...]
    e = jnp.exp(p - jnp.max(p, axis=0, keepdims=True))
    lb = e[0:1] / jnp.sum(e, axis=0, keepdims=True)
    z = acc
    t = jnp.exp(-jnp.abs(z))
    r = 1.0 / (1.0 + t)
    tr = t * r
    pos = z >= 0
    sig_p = jnp.where(pos, r, tr)
    sig_n = jnp.where(pos, tr, r)
    logf = jnp.log(lb + (1.0 - lb) * sig_p)
    kk = (1.0 - lb) * sig_n
    for h in range(HG_HEADS):
        sl = slice(h * HEAD_DIM, (h + 1) * HEAD_DIM)
        _store_heads(lf_ref, h, logf[:, sl])
        _store_heads(k_ref, h, kk[:, sl])


def _sigmoid_epilogue(acc, refs):
    (o_ref,) = refs
    o_ref[...] = jax.nn.sigmoid(acc).astype(BF)


def _proj_kernel(epilogue, h_ref, w_ref, *refs):
    epilogue(_dot(h_ref[...], w_ref[...]), refs)


def _proj(h, w, tn, epilogue, extra, extra_specs, out_shapes, out_specs, tm, name):
    m, d = h.shape
    ncols = w.shape[1]
    return pl.pallas_call(
        functools.partial(_proj_kernel, epilogue),
        grid=(m // tm, ncols // tn),
        in_specs=[
            pl.BlockSpec((tm, d), lambda i, j: (i, 0)),
            pl.BlockSpec((d, tn), lambda i, j: (0, j)),
        ] + extra_specs,
        out_specs=out_specs,
        out_shape=out_shapes,
        compiler_params=_params("arbitrary", "arbitrary"),
        name=name,
    )(h, w, *extra)


PROJ_TOKENS = 1024


def _in_projection(h, ws, batch, seq, rope_tabs, q_norm, k_norm, lb_f, lb_b, tag):
    m = h.shape[0]
    tm = PROJ_TOKENS
    rope = rope_tabs is not None
    if seq >= tm:
        tpb = seq // tm
        head_block = (1, tm)

        def head_index(i, j):
            return (i // tpb, 0, i % tpb, 0)
    else:
        tpb = 1
        head_block = (tm // seq, seq)

        def head_index(i, j):
            return (i, 0, 0, 0)

    def head_spec(nh):
        return pl.BlockSpec((head_block[0], nh, head_block[1], HEAD_DIM), head_index)

    def head_shape(nh, dt=BF):
        return jax.ShapeDtypeStruct((batch, nh, seq, HEAD_DIM), dt)

    vec_spec = pl.BlockSpec((1, HEAD_DIM), lambda i, j: (0, 0))
    tab_spec = pl.BlockSpec((tm, HEAD_DIM), lambda i, j: (i % tpb, 0))
    rope_in = list(rope_tabs) if rope else []
    rope_specs = [tab_spec, tab_spec] if rope else []

    q = _proj(h, ws["q"], ATT_WIDTH, functools.partial(_q_epilogue, rope),
              [q_norm] + rope_in, [vec_spec] + rope_specs,
              head_shape(ATT_HEADS), head_spec(ATT_HEADS), tm, "proj_q_" + tag)

    kv_shapes = [head_shape(ATT_KV_HEADS), head_shape(ATT_KV_HEADS)]
    kv_specs = [head_spec(ATT_KV_HEADS), head_spec(ATT_KV_HEADS)]
    if not rope:
        tok_spec = pl.BlockSpec((tm, KV_WIDTH), lambda i, j: (i, 0))
        kv_shapes += [jax.ShapeDtypeStruct((m, KV_WIDTH), F32)] * 2
        kv_specs += [tok_spec, tok_spec]
    kv = _proj(h, ws["kv"], 2 * KV_WIDTH, functools.partial(_kv_epilogue, rope),
               [k_norm] + rope_in, [vec_spec] + rope_specs,
               kv_shapes, kv_specs, tm, "proj_kv_" + tag)

    def heads(seg, act):
        return _proj(h, ws[seg], HG_WIDTH, functools.partial(_heads_epilogue, act),
                     [], [], head_shape(HG_HEADS), head_spec(HG_HEADS), tm, "proj_" + seg + "_" + tag)

    qh = heads("qh", "silu")
    vh = heads("vh", "none")
    og = heads("og", "silu")

    def forget(seg, lbp):
        return _proj(h, ws[seg], HG_WIDTH, _forget_epilogue,
                     [lbp], [pl.BlockSpec(lbp.shape, lambda i, j: (0, 0))],
                     [head_shape(HG_HEADS, F32), head_shape(HG_HEADS)],
                     [head_spec(HG_HEADS), head_spec(HG_HEADS)], tm, "proj_" + seg + "_" + tag)

    lf_f, k_f = forget("zf", lb_f)
    lf_b, k_b = forget("zb", lb_b)

    tn = 1024
    gates = _proj(h, ws["gates"], tn, _sigmoid_epilogue, [], [],
                  jax.ShapeDtypeStruct((m, 2 * D_MODEL), BF),
                  pl.BlockSpec((tm, tn), lambda i, j: (i, j)), tm, "proj_gates_" + tag)
    return q, kv, qh, vh, og, lf_f, k_f, lf_b, k_b, gates


def _attn_kernel(n_kv_src, q_ref, *refs):
    k_refs = refs[0:2 * n_kv_src:2]
    v_refs = refs[1:2 * n_kv_src:2]
    o_ref = refs[2 * n_kv_src]
    for g in range(GQA_GROUP):
        q = q_ref[0, g]
        s = [_dot_nt(q, k[0, 0]) for k in k_refs]
        mx = functools.reduce(jnp.maximum, [jnp.max(x, axis=-1, keepdims=True) for x in s])
        p = [jnp.exp(x - mx) for x in s]
        den = functools.reduce(lambda a, b: a + b, [jnp.sum(x, axis=-1, keepdims=True) for x in p])
        o = functools.reduce(lambda a, b: a + b,
                             [_dot(x.astype(BF), v[0, 0]) for x, v in zip(p, v_refs)])
        o_ref[0, :, g * HEAD_DIM:(g + 1) * HEAD_DIM] = (o / den).astype(BF)


def _attention(q, kvs, tq, name):
    batch, _, seq, _ = q.shape
    in_specs = [pl.BlockSpec((1, GQA_GROUP, tq, HEAD_DIM), lambda b, kv, t: (b, kv, t, 0))]
    for a in kvs:
        in_specs.append(pl.BlockSpec((1, 1, a.shape[2], HEAD_DIM), lambda b, kv, t: (b, kv, 0, 0)))
    return pl.pallas_call(
        functools.partial(_attn_kernel, len(kvs) // 2),
        grid=(batch, ATT_KV_HEADS, seq // tq),
        in_specs=in_specs,
        out_specs=pl.BlockSpec((1, tq, GQA_GROUP * HEAD_DIM), lambda b, kv, t: (b, t, kv)),
        out_shape=jax.ShapeDtypeStruct((batch, seq, ATT_WIDTH), BF),
        compiler_params=_params("arbitrary", "arbitrary", "arbitrary"),
        name=name,
    )(q, *kvs)


def _split3(x):
    hi = x.astype(BF)
    r1 = x - hi.astype(F32)
    mid = r1.astype(BF)
    lo = (r1 - mid.astype(F32)).astype(BF)
    return hi, mid, lo


def _hg_chunk(q, k, lf, v, st, tri_mask, tri_b, tot_row):
    hi, mid, lo = _split3(lf)
    b = _dot(tri_b, hi) + _dot(tri_b, mid) + _dot(tri_b, lo)
    btot = b[tot_row:tot_row + 1, :]
    kf = k.astype(F32)
    qd = (q.astype(F32) * jnp.exp(b)).astype(BF)
    kd = (kf * jnp.exp(-b)).astype(BF)
    ks = (kf * jnp.exp(btot - b)).astype(BF)
    a = jnp.where(tri_mask, _dot_nt(qd, kd), 0.0).astype(BF)
    o = _dot(a, v) + _dot_nt(qd, st.astype(BF))
    st_new = st * jnp.exp(btot) + _dot_tn(v, ks)
    return o, st_new


def _hgrn_kernel(seq, zero_init, q_ref, kf_ref, kb_ref, lf_ref, lb_ref, v_ref, og_ref, *refs):
    if zero_init:
        hn_ref, o_ref, sf_ref, sb_ref, of_scr, ob_scr = refs
        st_f0 = jnp.zeros((HEAD_DIM, HEAD_DIM), F32)
        st_b0 = jnp.zeros((HEAD_DIM, HEAD_DIM), F32)
    else:
        s0f_ref, s0b_ref, hn_ref, o_ref, of_scr, ob_scr = refs
        st_f0 = s0f_ref[0, 0].T
        st_b0 = s0b_ref[0, 0].T
    c = HG_CHUNK
    n = seq // c
    row = lax.broadcasted_iota(jnp.int32, (c, c), 0)
    col = lax.broadcasted_iota(jnp.int32, (c, c), 1)
    tril = col <= row
    triu = col >= row
    tril_b = jnp.where(tril, 1.0, 0.0).astype(BF)
    triu_b = jnp.where(triu, 1.0, 0.0).astype(BF)

    def body(i, carry):
        st_f, st_b = carry
        rf = pl.ds(pl.multiple_of(i * c, c), c)
        rb = pl.ds(pl.multiple_of((n - 1 - i) * c, c), c)
        o_f, st_f = _hg_chunk(q_ref[0, 0, rf, :], kf_ref[0, 0, rf, :], lf_ref[0, 0, rf, :],
                              v_ref[0, 0, rf, :], st_f, tril, tril_b, c - 1)
        of_scr[rf, :] = o_f
        o_b, st_b = _hg_chunk(q_ref[0, 0, rb, :], kb_ref[0, 0, rb, :], lb_ref[0, 0, rb, :],
                              v_ref[0, 0, rb, :], st_b, triu, triu_b, 0)
        ob_scr[rb, :] = o_b
        return st_f, st_b

    st_f, st_b = lax.fori_loop(0, n, body, (st_f0, st_b0))
    o = of_scr[...] + ob_scr[...]
    y = _rms(o) * hn_ref[...] * og_ref[0, 0].astype(F32)
    o_ref[0] = y.astype(BF)
    if zero_init:
        sf_ref[0, 0] = st_f.T
        sb_ref[0, 0] = st_b.T


def _hgrn(qh, k_f, k_b, lf_f, lf_b, vh, og, hg_norm, s0f, s0b, name):
    batch, heads, seq, hd = qh.shape
    zero_init = s0f is None
    head_spec = pl.BlockSpec((1, 1, seq, hd), lambda b, h: (b, h, 0, 0))
    state_spec = pl.BlockSpec((1, 1, hd, hd), lambda b, h: (b, h, 0, 0))
    vec_spec = pl.BlockSpec((1, hd), lambda b, h: (0, 0))
    o_spec = pl.BlockSpec((1, seq, hd), lambda b, h: (b, 0, h))
    o_shape = jax.ShapeDtypeStruct((batch, seq, heads * hd), BF)
    ins = [qh, k_f, k_b, lf_f, lf_b, vh, og]
    in_specs = [head_spec] * 7
    if zero_init:
        state_shape = jax.ShapeDtypeStruct((batch, heads, hd, hd), F32)
        out_shape = [o_shape, state_shape, state_shape]
        out_specs = [o_spec, state_spec, state_spec]
    else:
        ins += [s0f, s0b]
        in_specs += [state_spec, state_spec]
        out_shape = o_shape
        out_specs = o_spec
    ins.append(hg_norm)
    in_specs.append(vec_spec)
    return pl.pallas_call(
        functools.partial(_hgrn_kernel, seq, zero_init),
        grid=(batch, heads),
        in_specs=in_specs,
        out_specs=out_specs,
        out_shape=out_shape,
        scratch_shapes=[pltpu.VMEM((seq, hd), F32), pltpu.VMEM((seq, hd), F32)],
        compiler_params=_params("arbitrary", "arbitrary"),
        name=name,
    )(*ins)


def _merge_kernel(att_ref, hg_ref, ga_ref, gb_ref, x_ref, wa_ref, wh_ref, wo_ref,
                  g1_ref, sc2_ref, sh2_ref, npost_ref, npre_ref, x1_ref, h2_ref):
    a = _dot(att_ref[...], wa_ref[...])
    b = _dot(hg_ref[...], wh_ref[...])
    m = (ga_ref[...].astype(F32) * a + gb_ref[...].astype(F32) * b).astype(BF)
    mo = _dot(m, wo_ref[...])
    x1 = x_ref[...] + g1_ref[0] * (_rms(mo) * npost_ref[...])
    x1_ref[...] = x1
    h2 = _rms(x1) * npre_ref[...]
    h2_ref[...] = (h2 * (1.0 + sc2_ref[0]) + sh2_ref[0]).astype(BF)


def _merge(att, hg, gates, x, wa, wh, wo, g1, sc2, sh2, npost, npre, seq, name):
    m, d = x.shape
    tm = 256
    tpb = seq // tm
    w_att = att.shape[1]

    def const(shape):
        return pl.BlockSpec(shape, lambda i: (0, 0), pipeline_mode=pl.Buffered(1))

    mod_spec = pl.BlockSpec((1, 1, d), lambda i: (i // tpb, 0, 0))
    vec_spec = pl.BlockSpec((1, d), lambda i: (0, 0))
    return pl.pallas_call(
        _merge_kernel,
        grid=(m // tm,),
        in_specs=[
            pl.BlockSpec((tm, w_att), lambda i: (i, 0)),
            pl.BlockSpec((tm, w_att), lambda i: (i, 0)),
            pl.BlockSpec((tm, d), lambda i: (i, 0)),
            pl.BlockSpec((tm, d), lambda i: (i, 1)),
            pl.BlockSpec((tm, d), lambda i: (i, 0)),
            const(wa.shape), const(wh.shape), const(wo.shape),
            mod_spec, mod_spec, mod_spec, vec_spec, vec_spec,
        ],
        out_specs=[pl.BlockSpec((tm, d), lambda i: (i, 0)), pl.BlockSpec((tm, d), lambda i: (i, 0))],
        out_shape=[jax.ShapeDtypeStruct((m, d), F32), jax.ShapeDtypeStruct((m, d), BF)],
        compiler_params=_params("arbitrary"),
        name=name,
    )(att, hg, gates, gates, x, wa, wh, wo, g1, sc2, sh2, npost, npre)


def _ffn_kernel(nf, h_ref, wg_ref, wu_ref, wo_ref, x1_ref, g2_ref, npost_ref, o_ref, acc_ref):
    f = pl.program_id(1)
    h = h_ref[...]
    g = _dot(h, wg_ref[...])
    u = _dot(h, wu_ref[...])
    a = (g * jax.nn.sigmoid(g) * u).astype(BF)
    part = _dot(a, wo_ref[...])

    @pl.when(f == 0)
    def _():
        acc_ref[...] = part

    @pl.when(f > 0)
    def _():
        acc_ref[...] += part

    @pl.when(f == nf - 1)
    def _():
        y = _rms(acc_ref[...]) * npost_ref[...]
        o_ref[...] = x1_ref[...] + g2_ref[0] * y


def _ffn(h2, x1, w_in, w_out, g2, npost, seq, name):
    m, d = x1.shape
    tm, tf = 512, 512
    nf = D_FF // tf
    tpb = seq // tm
    return pl.pallas_call(
        functools.partial(_ffn_kernel, nf),
        grid=(m // tm, nf),
        in_specs=[
            pl.BlockSpec((tm, d), lambda i, f: (i, 0)),
            pl.BlockSpec((d, tf), lambda i, f: (0, f)),
            pl.BlockSpec((d, tf), lambda i, f: (0, nf + f)),
            pl.BlockSpec((tf, d), lambda i, f: (f, 0)),
            pl.BlockSpec((tm, d), lambda i, f: (i, 0)),
            pl.BlockSpec((1, 1, d), lambda i, f: (i // tpb, 0, 0)),
            pl.BlockSpec((1, d), lambda i, f: (0, 0)),
        ],
        out_specs=pl.BlockSpec((tm, d), lambda i, f: (i, 0)),
        out_shape=jax.ShapeDtypeStruct((m, d), F32),
        scratch_shapes=[pltpu.VMEM((tm, d), F32)],
        compiler_params=_params("arbitrary", "arbitrary"),
        name=name,
    )(h2, w_in, w_in, w_out, x1, g2, npost)


def _rope_tables(n_tokens):
    rows = n_tokens // GRID_W
    half = HEAD_DIM // 4
    r = jnp.repeat(jnp.arange(rows), GRID_W).astype(F32)
    col = jnp.tile(jnp.arange(GRID_W), rows).astype(F32)
    inv = ROPE_THETA ** (-jnp.arange(half, dtype=F32) / half)
    ar = r[:, None] * inv
    ac = col[:, None] * inv
    cos = jnp.concatenate([jnp.cos(ar), jnp.cos(ar), jnp.cos(ac), jnp.cos(ac)], axis=-1)
    sin = jnp.concatenate([-jnp.sin(ar), jnp.sin(ar), -jnp.sin(ac), jnp.sin(ac)], axis=-1)
    return cos, sin


def _layer(x3, mods, weights, rope_tabs, cache, states, tag):
    batch, seq, d = x3.shape
    m = batch * seq
    x = x3.reshape(m, d)
    (w_in, q_norm, k_norm, lb_f, lb_b, hg_norm, wa, wh, wo, w_ffn_in, w_ffn_out,
     n_pre_mix, n_post_mix, n_pre_ffn, n_post_ffn) = weights
    nb = mods.shape[0]
    sh1, sc1, g1, sh2, sc2, g2 = [mods[:, i].reshape(nb, 1, d) for i in range(6)]
    seq_mod = seq if nb == batch else m

    h = _modnorm(x, n_pre_mix, sc1, sh1, seq_mod)
    q, kv, qh, vh, og, lf_f, k_f, lf_b, k_b, gates = _in_projection(
        h, w_in, batch, seq, rope_tabs, q_norm, k_norm, lb_f, lb_b, tag)

    if cache is None:
        k_bf, v_bf, k_f32, v_f32 = kv
        att = _attention(q, [k_bf, v_bf], seq, "attn_" + tag)
    else:
        k_bf, v_bf = kv
        att = _attention(q, [k_bf, v_bf, cache[0], cache[1]], 256, "attn_" + tag)
        k_f32 = v_f32 = None

    if states is None:
        hg, s_f, s_b = _hgrn(qh, k_f, k_b, lf_f, lf_b, vh, og, hg_norm, None, None, "hgrn_" + tag)
    else:
        hg = _hgrn(qh, k_f, k_b, lf_f, lf_b, vh, og, hg_norm, states[0], states[1], "hgrn_" + tag)
        s_f = s_b = None

    x1, h2 = _merge(att.reshape(m, ATT_WIDTH), hg.reshape(m, HG_WIDTH), gates, x, wa, wh, wo,
                    g1, sc2, sh2, n_post_mix, n_pre_ffn, seq_mod, "merge_" + tag)
    y = _ffn(h2, x1, w_ffn_in, w_ffn_out, g2, n_post_ffn, seq_mod, "ffn_" + tag)
    return y.reshape(batch, seq, d), k_f32, v_f32, s_f, s_b


def kernel(x_prompt, x_sample, cache_k, cache_v, state_fwd, state_bwd, c, c_ctx, w_ada, b_ada,
           norm_pre_mix, norm_post_mix, norm_pre_ffn, norm_post_ffn, w_in, q_norm, k_norm,
           lb_fwd, lb_bwd, hg_norm, w_br_att, w_br_hg, w_out, w_ffn_in, w_ffn_out):
    depth = w_in.shape[0]
    assert depth == 1 and lb_fwd.shape[0] == 2
    batch, seq, d = x_prompt.shape
    dec_batch, dec_seq, _ = x_sample.shape
    past = cache_k.shape[2]

    cond = jnp.concatenate([c_ctx[None, :], c, jnp.zeros((8 - 1 - dec_batch, d), F32)], axis=0)
    mods = _ada(cond, w_ada[0], b_ada[0][None, :]).reshape(8, 6, d)

    segments = (("q", OFF_Q, OFF_KV), ("kv", OFF_KV, OFF_QH), ("qh", OFF_QH, OFF_ZF),
                ("zf", OFF_ZF, OFF_ZB), ("zb", OFF_ZB, OFF_VH), ("vh", OFF_VH, OFF_OG),
                ("og", OFF_OG, OFF_GA), ("gates", OFF_GA, OFF_GB + D_MODEL))
    w_in_segments = {name: w_in[0, :, a:b].astype(BF) for name, a, b in segments}
    weights = (
        w_in_segments, q_norm, k_norm, lb_fwd, lb_bwd, hg_norm,
        w_br_att[0].astype(BF), w_br_hg[0].astype(BF), w_out[0].astype(BF),
        w_ffn_in[0].astype(BF), w_ffn_out[0].astype(BF),
        norm_pre_mix, norm_post_mix, norm_pre_ffn, norm_post_ffn,
    )

    y_p, k_c, v_c, s_f, s_b = _layer(x_prompt, mods[0:1], weights, None, None, None, "ctx")

    cache = (
        cache_k[:, 0].transpose(0, 2, 1, 3).astype(BF),
        cache_v[:, 0].transpose(0, 2, 1, 3).astype(BF),
    )
    states = (state_fwd[:, 0], state_bwd[:, 0])
    y_s, _, _, _, _ = _layer(x_sample, mods[1:1 + dec_batch], weights, _rope_tables(dec_seq),
                             cache, states, "lat")

    new_k = k_c.reshape(batch, 1, seq, ATT_KV_HEADS, HEAD_DIM)
    new_v = v_c.reshape(batch, 1, seq, ATT_KV_HEADS, HEAD_DIM)
    return (y_p, y_s, new_k, new_v, s_f[:, None], s_b[:, None])
```

```python
import functools

import jax
import jax.numpy as jnp
from jax import lax
from jax.experimental import pallas as pl
from jax.experimental.pallas import tpu as pltpu

D_MODEL = 2048
HEAD_DIM = 128
ATT_HEADS = 8
ATT_KV_HEADS = 2
GQA_GROUP = ATT_HEADS // ATT_KV_HEADS
ATT_WIDTH = ATT_HEADS * HEAD_DIM
KV_WIDTH = ATT_KV_HEADS * HEAD_DIM
HG_HEADS = 8
HG_WIDTH = HG_HEADS * HEAD_DIM
D_FF = 5632
GRID_W = 64
ROPE_THETA = 10000.0
NORM_EPS = 1e-6
HG_CHUNK = 128
HG_DIAG_LEVEL = 31

OFF_Q = 0
OFF_KV = ATT_WIDTH
OFF_QH = OFF_KV + 2 * KV_WIDTH
OFF_ZF = OFF_QH + HG_WIDTH
OFF_ZB = OFF_ZF + HG_WIDTH
OFF_VH = OFF_ZB + HG_WIDTH
OFF_OG = OFF_VH + HG_WIDTH
OFF_GA = OFF_OG + HG_WIDTH
OFF_GB = OFF_GA + D_MODEL

BF = jnp.bfloat16
F32 = jnp.float32

VMEM_LIMIT_BYTES = 56 * 1024 * 1024


def _params(*sem):
    return pltpu.CompilerParams(dimension_semantics=sem, vmem_limit_bytes=VMEM_LIMIT_BYTES)


def _rms(x):
    return x * lax.rsqrt(jnp.mean(x * x, axis=-1, keepdims=True) + NORM_EPS)


def _dot(a, b):
    return jnp.dot(a, b, preferred_element_type=F32)


def _dot_nt(a, b):
    return lax.dot_general(a, b, (((1,), (1,)), ((), ())), preferred_element_type=F32)


def _dot_tn(a, b):
    return lax.dot_general(a, b, (((0,), (0,)), ((), ())), preferred_element_type=F32)


def _ada_kernel(c_ref, w_ref, b_ref, o_ref):
    c = c_ref[...]
    s = (c * jax.nn.sigmoid(c)).astype(BF)
    o_ref[...] = _dot(s, w_ref[...].astype(BF)) + b_ref[...]


def _ada(cond, w, b):
    rows, d = cond.shape
    n = w.shape[1]
    tn = 1536
    return pl.pallas_call(
        _ada_kernel,
        grid=(n // tn,),
        in_specs=[
            pl.BlockSpec((rows, d), lambda j: (0, 0)),
            pl.BlockSpec((d, tn), lambda j: (0, j)),
            pl.BlockSpec((1, tn), lambda j: (0, j)),
        ],
        out_specs=pl.BlockSpec((rows, tn), lambda j: (0, j)),
        out_shape=jax.ShapeDtypeStruct((rows, n), F32),
        compiler_params=_params("arbitrary"),
        name="ada_mod",
    )(cond, w, b)


def _modnorm_kernel(x_ref, g_ref, sc_ref, sh_ref, o_ref):
    y = _rms(x_ref[...]) * g_ref[...]
    o_ref[...] = (y * (1.0 + sc_ref[0]) + sh_ref[0]).astype(BF)


def _modnorm(x, g, sc, sh, seq):
    m, d = x.shape
    tm = 512
    tpb = seq // tm
    return pl.pallas_call(
        _modnorm_kernel,
        grid=(m // tm,),
        in_specs=[
            pl.BlockSpec((tm, d), lambda i: (i, 0)),
            pl.BlockSpec((1, d), lambda i: (0, 0)),
            pl.BlockSpec((1, 1, d), lambda i: (i // tpb, 0, 0)),
            pl.BlockSpec((1, 1, d), lambda i: (i // tpb, 0, 0)),
        ],
        out_specs=pl.BlockSpec((tm, d), lambda i: (i, 0)),
        out_shape=jax.ShapeDtypeStruct((m, d), BF),
        compiler_params=_params("arbitrary"),
        name="modnorm",
    )(x, g, sc, sh)


def _rope(y, cos_ref, sin_ref):
    lane = lax.broadcasted_iota(jnp.int32, y.shape, 1)
    first = (lane & 32) == 0
    swapped = jnp.where(first, pltpu.roll(y, 96, 1), pltpu.roll(y, 32, 1))
    return y * cos_ref[...] + swapped * sin_ref[...]


def _store_heads(o_ref, h, y):
    nbb, _, ts, _ = o_ref.shape
    for bl in range(nbb):
        o_ref[bl, h] = y[bl * ts:(bl + 1) * ts].astype(o_ref.dtype)


def _q_epilogue(rope, acc, refs):
    if rope:
        qn_ref, cos_ref, sin_ref, o_ref = refs
    else:
        qn_ref, o_ref = refs
    scale = HEAD_DIM ** -0.5
    for h in range(ATT_HEADS):
        y = _rms(acc[:, h * HEAD_DIM:(h + 1) * HEAD_DIM]) * qn_ref[...]
        if rope:
            y = _rope(y, cos_ref, sin_ref)
        _store_heads(o_ref, h, y * scale)


def _kv_epilogue(rope, acc, refs):
    if rope:
        kn_ref, cos_ref, sin_ref, kb_ref, vb_ref = refs
    else:
        kn_ref, kb_ref, vb_ref, kf_ref, vf_ref = refs
    for h in range(ATT_KV_HEADS):
        sl = slice(h * HEAD_DIM, (h + 1) * HEAD_DIM)
        k = _rms(acc[:, sl]) * kn_ref[...]
        v = acc[:, KV_WIDTH + h * HEAD_DIM:KV_WIDTH + (h + 1) * HEAD_DIM]
        if rope:
            k = _rope(k, cos_ref, sin_ref)
        else:
            kf_ref[:, sl] = k
            vf_ref[:, sl] = v
        _store_heads(kb_ref, h, k)
        _store_heads(vb_ref, h, v)


def _heads_epilogue(act, acc, refs):
    (o_ref,) = refs
    for h in range(HG_HEADS):
        y = acc[:, h * HEAD_DIM:(h + 1) * HEAD_DIM]
        if act == "silu":
            y = y * jax.nn.sigmoid(y)
        _store_heads(o_ref, h, y)


def _forget_epilogue(acc, refs):
    lbp_ref, lf_ref, k_ref = refs
    p = lbp_ref[...]
    e = jnp.exp(p - jnp.max(p, axis=0, keepdims=True))
    lb = e[0:1] / jnp.sum(e, axis=0, keepdims=True)
    z = acc
    t = jnp.exp(-jnp.abs(z))
    r = 1.0 / (1.0 + t)
    tr = t * r
    pos = z >= 0
    sig_p = jnp.where(pos, r, tr)
    sig_n = jnp.where(pos, tr, r)
    logf = jnp.log(lb + (1.0 - lb) * sig_p)
    kk = (1.0 - lb) * sig_n
    for h in range(HG_HEADS):
        sl = slice(h * HEAD_DIM, (h + 1) * HEAD_DIM)
        _store_heads(lf_ref, h, logf[:, sl])
        _store_heads(k_ref, h, kk[:, sl])


def _sigmoid_epilogue(acc, refs):
    (o_ref,) = refs
    o_ref[...] = jax.nn.sigmoid(acc).astype(BF)


def _proj_kernel(epilogue, h_ref, w_ref, *refs):
    epilogue(_dot(h_ref[...], w_ref[...]), refs)


def _proj(h, w, tn, epilogue, extra, extra_specs, out_shapes, out_specs, tm, name):
    m, d = h.shape
    ncols = w.shape[1]
    return pl.pallas_call(
        functools.partial(_proj_kernel, epilogue),
        grid=(m // tm, ncols // tn),
        in_specs=[
            pl.BlockSpec((tm, d), lambda i, j: (i, 0)),
            pl.BlockSpec((d, tn), lambda i, j: (0, j)),
        ] + extra_specs,
        out_specs=out_specs,
        out_shape=out_shapes,
        compiler_params=_params("arbitrary", "arbitrary"),
        name=name,
    )(h, w, *extra)


PROJ_TOKENS = 1024


def _in_projection(h, ws, batch, seq, rope_tabs, q_norm, k_norm, lb_f, lb_b, tag):
    m = h.shape[0]
    tm = PROJ_TOKENS
    rope = rope_tabs is not None
    if seq >= tm:
        tpb = seq // tm
        head_block = (1, tm)

        def head_index(i, j):
            return (i // tpb, 0, i % tpb, 0)
    else:
        tpb = 1
        head_block = (tm // seq, seq)

        def head_index(i, j):
            return (i, 0, 0, 0)

    def head_spec(nh):
        return pl.BlockSpec((head_block[0], nh, head_block[1], HEAD_DIM), head_index)

    def head_shape(nh, dt=BF):
        return jax.ShapeDtypeStruct((batch, nh, seq, HEAD_DIM), dt)

    vec_spec = pl.BlockSpec((1, HEAD_DIM), lambda i, j: (0, 0))
    tab_spec = pl.BlockSpec((tm, HEAD_DIM), lambda i, j: (i % tpb, 0))
    rope_in = list(rope_tabs) if rope else []
    rope_specs = [tab_spec, tab_spec] if rope else []

    q = _proj(h, ws["q"], ATT_WIDTH, functools.partial(_q_epilogue, rope),
              [q_norm] + rope_in, [vec_spec] + rope_specs,
              head_shape(ATT_HEADS), head_spec(ATT_HEADS), tm, "proj_q_" + tag)

    kv_shapes = [head_shape(ATT_KV_HEADS), head_shape(ATT_KV_HEADS)]
    kv_specs = [head_spec(ATT_KV_HEADS), head_spec(ATT_KV_HEADS)]
    if not rope:
        tok_spec = pl.BlockSpec((tm, KV_WIDTH), lambda i, j: (i, 0))
        kv_shapes += [jax.ShapeDtypeStruct((m, KV_WIDTH), F32)] * 2
        kv_specs += [tok_spec, tok_spec]
    kv = _proj(h, ws["kv"], 2 * KV_WIDTH, functools.partial(_kv_epilogue, rope),
               [k_norm] + rope_in, [vec_spec] + rope_specs,
               kv_shapes, kv_specs, tm, "proj_kv_" + tag)

    def heads(seg, act):
        return _proj(h, ws[seg], HG_WIDTH, functools.partial(_heads_epilogue, act),
                     [], [], head_shape(HG_HEADS), head_spec(HG_HEADS), tm, "proj_" + seg + "_" + tag)

    qh = heads("qh", "silu")
    vh = heads("vh", "none")
    og = heads("og", "silu")

    def forget(seg, lbp):
        return _proj(h, ws[seg], HG_WIDTH, _forget_epilogue,
                     [lbp], [pl.BlockSpec(lbp.shape, lambda i, j: (0, 0))],
                     [head_shape(HG_HEADS, F32), head_shape(HG_HEADS)],
                     [head_spec(HG_HEADS), head_spec(HG_HEADS)], tm, "proj_" + seg + "_" + tag)

    lf_f, k_f = forget("zf", lb_f)
    lf_b, k_b = forget("zb", lb_b)

    tn = 1024
    gates = _proj(h, ws["gates"], tn, _sigmoid_epilogue, [], [],
                  jax.ShapeDtypeStruct((m, 2 * D_MODEL), BF),
                  pl.BlockSpec((tm, tn), lambda i, j: (i, j)), tm, "proj_gates_" + tag)
    return q, kv, qh, vh, og, lf_f, k_f, lf_b, k_b, gates


def _attn_kernel(n_kv_src, q_ref, *refs):
    k_refs = refs[0:2 * n_kv_src:2]
    v_refs = refs[1:2 * n_kv_src:2]
    o_ref = refs[2 * n_kv_src]
    for g in range(GQA_GROUP):
        q = q_ref[0, g]
        s = [_dot_nt(q, k[0, 0]) for k in k_refs]
        mx = functools.reduce(jnp.maximum, [jnp.max(x, axis=-1, keepdims=True) for x in s])
        p = [jnp.exp(x - mx) for x in s]
        den = functools.reduce(lambda a, b: a + b, [jnp.sum(x, axis=-1, keepdims=True) for x in p])
        o = functools.reduce(lambda a, b: a + b,
                             [_dot(x.astype(BF), v[0, 0]) for x, v in zip(p, v_refs)])
        o_ref[0, :, g * HEAD_DIM:(g + 1) * HEAD_DIM] = (o / den).astype(BF)


def _attention(q, kvs, tq, name):
    batch, _, seq, _ = q.shape
    in_specs = [pl.BlockSpec((1, GQA_GROUP, tq, HEAD_DIM), lambda b, kv, t: (b, kv, t, 0))]
    for a in kvs:
        in_specs.append(pl.BlockSpec((1, 1, a.shape[2], HEAD_DIM), lambda b, kv, t: (b, kv, 0, 0)))
    return pl.pallas_call(
        functools.partial(_attn_kernel, len(kvs) // 2),
        grid=(batch, ATT_KV_HEADS, seq // tq),
        in_specs=in_specs,
        out_specs=pl.BlockSpec((1, tq, GQA_GROUP * HEAD_DIM), lambda b, kv, t: (b, t, kv)),
        out_shape=jax.ShapeDtypeStruct((batch, seq, ATT_WIDTH), BF),
        compiler_params=_params("arbitrary", "arbitrary", "arbitrary"),
        name=name,
    )(q, *kvs)


def _split3(x):
    hi = x.astype(BF)
    r1 = x - hi.astype(F32)
    mid = r1.astype(BF)
    lo = (r1 - mid.astype(F32)).astype(BF)
    return hi, mid, lo


def _block_reference(b3, hb, rev):
    n8 = b3.shape[0]
    if hb < 8:
        def row(r):
            return jnp.broadcast_to(b3[:, r:r + 1, :], b3.shape)
        if hb == 4:
            return row(4 if rev else 3)
        assert hb == 2
        sub = lax.broadcasted_iota(jnp.int32, b3.shape, 1)
        r0, r1 = (2, 6) if rev else (1, 5)
        return jnp.where(sub < 4, row(r0), row(r1))
    g = hb // 8
    nblk = n8 // (2 * g)
    edge = (b3[:, 0:1, :] if rev else b3[:, 7:8, :]).reshape(nblk, 2 * g, 1, b3.shape[2])
    pick = edge[:, g:g + 1] if rev else edge[:, g - 1:g]
    return jnp.broadcast_to(pick, (nblk, 2 * g, 8, b3.shape[2])).reshape(b3.shape)


def _hg_chunk(q, k, lf, vt, st, tri_b, level, rev):
    c = q.shape[0]
    hi, mid, lo = _split3(lf)
    b = _dot(tri_b, hi) + _dot(tri_b, mid) + _dot(tri_b, lo)
    b3 = b.reshape(c // 8, 8, HEAD_DIM)
    sub = lax.broadcasted_iota(jnp.int32, (c, HEAD_DIM), 0)
    a = jnp.where(level == HG_DIAG_LEVEL, _dot_nt(q, k), 0.0)
    hb, j = 1, 0
    while hb < c:
        if hb == 1:
            arg = jnp.where((sub & 1) == (0 if rev else 1), lf, 0.0)
        else:
            arg = -jnp.abs(b - _block_reference(b3, hb, rev).reshape(c, HEAD_DIM))
        e = jnp.exp(arg).astype(BF)
        a = jnp.where(level == j, _dot_nt(q * e, k * e), a)
        hb, j = 2 * hb, j + 1
    btot = b[0:1, :] if rev else b[c - 1:c, :]
    qs = q * jnp.exp(b).astype(BF)
    ks = k * jnp.exp(btot - b).astype(BF)
    o = _dot_nt(jnp.concatenate([a.astype(BF), qs], axis=1),
                jnp.concatenate([vt, st.astype(BF)], axis=1))
    st_new = st * jnp.exp(btot) + _dot(vt, ks)
    return o, st_new


def _pair_levels(c, rev):
    t = lax.broadcasted_iota(jnp.int32, (c, c), 0)
    s = lax.broadcasted_iota(jnp.int32, (c, c), 1)
    x = t ^ s
    lvl = jnp.zeros((c, c), jnp.int32)
    p = 2
    while p < c:
        lvl = lvl + jnp.where(x >= p, 1, 0)
        p *= 2
    valid = (s > t) if rev else (s < t)
    return jnp.where(valid, lvl, jnp.where(s == t, HG_DIAG_LEVEL, -1))


def _hgrn_kernel(seq, heads, zero_init, q_ref, kf_ref, kb_ref, lf_ref, lb_ref, v_ref, og_ref, *refs):
    if zero_init:
        hn_ref, o_ref, sf_ref, sb_ref, vt_scr, of_scr, ob_scr = refs
    else:
        s0f_ref, s0b_ref, hn_ref, o_ref, vt_scr, of_scr, ob_scr = refs
    c = HG_CHUNK
    n = seq // c
    t = lax.broadcasted_iota(jnp.int32, (c, c), 0)
    s = lax.broadcasted_iota(jnp.int32, (c, c), 1)
    tril_b = jnp.where(s <= t, 1.0, 0.0).astype(BF)
    triu_b = jnp.where(s >= t, 1.0, 0.0).astype(BF)
    level_f = _pair_levels(c, False)
    level_b = _pair_levels(c, True)

    def rows(i):
        return pl.ds(pl.multiple_of(i * c, c), c)

    def head_body(h, _):
        def transpose_v(i, _):
            vt_scr[i] = v_ref[0, h, rows(i), :].T
            return 0

        lax.fori_loop(0, n, transpose_v, 0)

        def body(i, carry):
            st_f, st_b = carry
            rf = rows(i)
            rb = rows(n - 1 - i)
            o_f, st_f = _hg_chunk(q_ref[0, h, rf, :], kf_ref[0, h, rf, :], lf_ref[0, h, rf, :],
                                  vt_scr[i], st_f, tril_b, level_f, False)
            of_scr[rf, :] = o_f
            o_b, st_b = _hg_chunk(q_ref[0, h, rb, :], kb_ref[0, h, rb, :], lb_ref[0, h, rb, :],
                                  vt_scr[n - 1 - i], st_b, triu_b, level_b, True)
            ob_scr[rb, :] = o_b
            return st_f, st_b

        if zero_init:
            st0 = (jnp.zeros((HEAD_DIM, HEAD_DIM), F32), jnp.zeros((HEAD_DIM, HEAD_DIM), F32))
        else:
            st0 = (s0f_ref[0, h].T, s0b_ref[0, h].T)
        st_f, st_b = lax.fori_loop(0, n, body, st0, unroll=2)
        o = of_scr[...] + ob_scr[...]
        y = _rms(o) * hn_ref[...] * og_ref[0, h].astype(F32)
        o_ref[0, h] = y.astype(BF)
        if zero_init:
            sf_ref[0, h] = st_f.T
            sb_ref[0, h] = st_b.T
        return 0

    lax.fori_loop(0, heads, head_body, 0)


def _hgrn(qh, k_f, k_b, lf_f, lf_b, vh, og, hg_norm, s0f, s0b, heads_per_step, name):
    batch, heads, seq, hd = qh.shape
    hps = heads_per_step
    zero_init = s0f is None
    head_spec = pl.BlockSpec((1, hps, seq, hd), lambda b, h: (b, h, 0, 0))
    state_spec = pl.BlockSpec((1, hps, hd, hd), lambda b, h: (b, h, 0, 0))
    vec_spec = pl.BlockSpec((1, hd), lambda b, h: (0, 0))
    o_shape = jax.ShapeDtypeStruct((batch, heads, seq, hd), BF)
    ins = [qh, k_f, k_b, lf_f, lf_b, vh, og]
    in_specs = [head_spec] * 7
    if zero_init:
        state_shape = jax.ShapeDtypeStruct((batch, heads, hd, hd), F32)
        out_shape = [o_shape, state_shape, state_shape]
        out_specs = [head_spec, state_spec, state_spec]
    else:
        ins += [s0f, s0b]
        in_specs += [state_spec, state_spec]
        out_shape = o_shape
        out_specs = head_spec
    ins.append(hg_norm)
    in_specs.append(vec_spec)
    return pl.pallas_call(
        functools.partial(_hgrn_kernel, seq, hps, zero_init),
        grid=(batch, heads // hps),
        in_specs=in_specs,
        out_specs=out_specs,
        out_shape=out_shape,
        scratch_shapes=[pltpu.VMEM((seq // HG_CHUNK, hd, HG_CHUNK), BF),
                        pltpu.VMEM((seq, hd), F32), pltpu.VMEM((seq, hd), F32)],
        compiler_params=_params("arbitrary", "arbitrary"),
        name=name,
    )(*ins)


def _merge_kernel(att_ref, hg_ref, ga_ref, gb_ref, x_ref, wa_ref, wh_ref, wo_ref,
                  g1_ref, sc2_ref, sh2_ref, npost_ref, npre_ref, x1_ref, h2_ref):
    a = _dot(att_ref[...], wa_ref[...])
    hg = jnp.concatenate([hg_ref[0, h] for h in range(HG_HEADS)], axis=1)
    b = _dot(hg, wh_ref[...])
    m = (ga_ref[...].astype(F32) * a + gb_ref[...].astype(F32) * b).astype(BF)
    mo = _dot(m, wo_ref[...])
    x1 = x_ref[...] + g1_ref[0] * (_rms(mo) * npost_ref[...])
    x1_ref[...] = x1
    h2 = _rms(x1) * npre_ref[...]
    h2_ref[...] = (h2 * (1.0 + sc2_ref[0]) + sh2_ref[0]).astype(BF)


def _merge(att, hg, gates, x, wa, wh, wo, g1, sc2, sh2, npost, npre, seq, name):
    m, d = x.shape
    tm = 256
    tpb = seq // tm
    w_att = att.shape[1]
    hg_tpb = hg.shape[2] // tm
    hg_spec = pl.BlockSpec((1, HG_HEADS, tm, HEAD_DIM), lambda i: (i // hg_tpb, 0, i % hg_tpb, 0))

    def const(shape):
        return pl.BlockSpec(shape, lambda i: (0, 0), pipeline_mode=pl.Buffered(1))

    mod_spec = pl.BlockSpec((1, 1, d), lambda i: (i // tpb, 0, 0))
    vec_spec = pl.BlockSpec((1, d), lambda i: (0, 0))
    return pl.pallas_call(
        _merge_kernel,
        grid=(m // tm,),
        in_specs=[
            pl.BlockSpec((tm, w_att), lambda i: (i, 0)),
            hg_spec,
            pl.BlockSpec((tm, d), lambda i: (i, 0)),
            pl.BlockSpec((tm, d), lambda i: (i, 1)),
            pl.BlockSpec((tm, d), lambda i: (i, 0)),
            const(wa.shape), const(wh.shape), const(wo.shape),
            mod_spec, mod_spec, mod_spec, vec_spec, vec_spec,
        ],
        out_specs=[pl.BlockSpec((tm, d), lambda i: (i, 0)), pl.BlockSpec((tm, d), lambda i: (i, 0))],
        out_shape=[jax.ShapeDtypeStruct((m, d), F32), jax.ShapeDtypeStruct((m, d), BF)],
        compiler_params=_params("arbitrary"),
        name=name,
    )(att, hg, gates, gates, x, wa, wh, wo, g1, sc2, sh2, npost, npre)


def _ffn_up_kernel(h_ref, wg_ref, wu_ref, a_ref):
    h = h_ref[...]
    g = _dot(h, wg_ref[...])
    u = _dot(h, wu_ref[...])
    a_ref[...] = (g * jax.nn.sigmoid(g) * u).astype(BF)


def _ffn_down_kernel(a_ref, wo_ref, x1_ref, g2_ref, npost_ref, o_ref):
    y = _rms(_dot(a_ref[...], wo_ref[...])) * npost_ref[...]
    o_ref[...] = x1_ref[...] + g2_ref[0] * y


def _ffn(h2, x1, w_in, w_out, g2, npost, seq, name):
    m, d = x1.shape
    tm, tf = PROJ_TOKENS, 512
    nf = D_FF // tf
    act = pl.pallas_call(
        _ffn_up_kernel,
        grid=(m // tm, nf),
        in_specs=[
            pl.BlockSpec((tm, d), lambda i, f: (i, 0)),
            pl.BlockSpec((d, tf), lambda i, f: (0, f)),
            pl.BlockSpec((d, tf), lambda i, f: (0, nf + f)),
        ],
        out_specs=pl.BlockSpec((tm, tf), lambda i, f: (i, f)),
        out_shape=jax.ShapeDtypeStruct((m, D_FF), BF),
        compiler_params=_params("arbitrary", "arbitrary"),
        name=name + "_up",
    )(h2, w_in, w_in)
    td = 256
    tpb = seq // td
    return pl.pallas_call(
        _ffn_down_kernel,
        grid=(m // td,),
        in_specs=[
            pl.BlockSpec((td, D_FF), lambda i: (i, 0)),
            pl.BlockSpec((D_FF, d), lambda i: (0, 0), pipeline_mode=pl.Buffered(1)),
            pl.BlockSpec((td, d), lambda i: (i, 0)),
            pl.BlockSpec((1, 1, d), lambda i: (i // tpb, 0, 0)),
            pl.BlockSpec((1, d), lambda i: (0, 0)),
        ],
        out_specs=pl.BlockSpec((td, d), lambda i: (i, 0)),
        out_shape=jax.ShapeDtypeStruct((m, d), F32),
        compiler_params=_params("arbitrary"),
        name=name + "_down",
    )(act, w_out, x1, g2, npost)


def _rope_tables(n_tokens):
    rows = n_tokens // GRID_W
    half = HEAD_DIM // 4
    r = jnp.repeat(jnp.arange(rows), GRID_W).astype(F32)
    col = jnp.tile(jnp.arange(GRID_W), rows).astype(F32)
    inv = ROPE_THETA ** (-jnp.arange(half, dtype=F32) / half)
    ar = r[:, None] * inv
    ac = col[:, None] * inv
    cos = jnp.concatenate([jnp.cos(ar), jnp.cos(ar), jnp.cos(ac), jnp.cos(ac)], axis=-1)
    sin = jnp.concatenate([-jnp.sin(ar), jnp.sin(ar), -jnp.sin(ac), jnp.sin(ac)], axis=-1)
    return cos, sin


def _layer(x3, mods, weights, rope_tabs, cache, states, tag):
    batch, seq, d = x3.shape
    m = batch * seq
    x = x3.reshape(m, d)
    (w_in, q_norm, k_norm, lb_f, lb_b, hg_norm, wa, wh, wo, w_ffn_in, w_ffn_out,
     n_pre_mix, n_post_mix, n_pre_ffn, n_post_ffn) = weights
    nb = mods.shape[0]
    sh1, sc1, g1, sh2, sc2, g2 = [mods[:, i].reshape(nb, 1, d) for i in range(6)]
    seq_mod = seq if nb == batch else m

    h = _modnorm(x, n_pre_mix, sc1, sh1, seq_mod)
    q, kv, qh, vh, og, lf_f, k_f, lf_b, k_b, gates = _in_projection(
        h, w_in, batch, seq, rope_tabs, q_norm, k_norm, lb_f, lb_b, tag)

    if cache is None:
        k_bf, v_bf, k_f32, v_f32 = kv
        att = _attention(q, [k_bf, v_bf], seq, "attn_" + tag)
    else:
        k_bf, v_bf = kv
        att = _attention(q, [k_bf, v_bf, cache[0], cache[1]], 256, "attn_" + tag)
        k_f32 = v_f32 = None

    hps = HG_HEADS if seq <= 2 * HG_CHUNK else 1
    if states is None:
        hg, s_f, s_b = _hgrn(qh, k_f, k_b, lf_f, lf_b, vh, og, hg_norm, None, None, hps, "hgrn_" + tag)
    else:
        hg = _hgrn(qh, k_f, k_b, lf_f, lf_b, vh, og, hg_norm, states[0], states[1], hps, "hgrn_" + tag)
        s_f = s_b = None

    x1, h2 = _merge(att.reshape(m, ATT_WIDTH), hg, gates, x, wa, wh, wo,
                    g1, sc2, sh2, n_post_mix, n_pre_ffn, seq_mod, "merge_" + tag)
    y = _ffn(h2, x1, w_ffn_in, w_ffn_out, g2, n_post_ffn, seq_mod, "ffn_" + tag)
    return y.reshape(batch, seq, d), k_f32, v_f32, s_f, s_b


def kernel(x_prompt, x_sample, cache_k, cache_v, state_fwd, state_bwd, c, c_ctx, w_ada, b_ada,
           norm_pre_mix, norm_post_mix, norm_pre_ffn, norm_post_ffn, w_in, q_norm, k_norm,
           lb_fwd, lb_bwd, hg_norm, w_br_att, w_br_hg, w_out, w_ffn_in, w_ffn_out):
    depth = w_in.shape[0]
    assert depth == 1 and lb_fwd.shape[0] == 2
    batch, seq, d = x_prompt.shape
    dec_batch, dec_seq, _ = x_sample.shape
    past = cache_k.shape[2]

    cond = jnp.concatenate([c_ctx[None, :], c, jnp.zeros((8 - 1 - dec_batch, d), F32)], axis=0)
    mods = _ada(cond, w_ada[0], b_ada[0][None, :]).reshape(8, 6, d)

    segments = (("q", OFF_Q, OFF_KV), ("kv", OFF_KV, OFF_QH), ("qh", OFF_QH, OFF_ZF),
                ("zf", OFF_ZF, OFF_ZB), ("zb", OFF_ZB, OFF_VH), ("vh", OFF_VH, OFF_OG),
                ("og", OFF_OG, OFF_GA), ("gates", OFF_GA, OFF_GB + D_MODEL))
    w_in_segments = {name: w_in[0, :, a:b].astype(BF) for name, a, b in segments}
    weights = (
        w_in_segments, q_norm, k_norm, lb_fwd, lb_bwd, hg_norm,
        w_br_att[0].astype(BF), w_br_hg[0].astype(BF), w_out[0].astype(BF),
        w_ffn_in[0].astype(BF), w_ffn_out[0].astype(BF),
        norm_pre_mix, norm_post_mix, norm_pre_ffn, norm_post_ffn,
    )

    y_p, k_c, v_c, s_f, s_b = _layer(x_prompt, mods[0:1], weights, None, None, None, "ctx")

    cache = (
        cache_k[:, 0].transpose(0, 2, 1, 3).astype(BF),
        cache_v[:, 0].transpose(0, 2, 1, 3).astype(BF),
    )
    states = (state_fwd[:, 0], state_bwd[:, 0])
    y_s, _, _, _, _ = _layer(x_sample, mods[1:1 + dec_batch], weights, _rope_tables(dec_seq),
                             cache, states, "lat")

    new_k = k_c.reshape(batch, 1, seq, ATT_KV_HEADS, HEAD_DIM)
    new_v = v_c.reshape(batch, 1, seq, ATT_KV_HEADS, HEAD_DIM)
    return (y_p, y_s, new_k, new_v, s_f[:, None], s_b[:, None])
```

```python
import functools

import jax
import jax.numpy as jnp
from jax import lax
from jax.experimental import pallas as pl
from jax.experimental.pallas import tpu as pltpu

D_MODEL = 2048
HEAD_DIM = 128
ATT_HEADS = 8
ATT_KV_HEADS = 2
GQA_GROUP = ATT_HEADS // ATT_KV_HEADS
ATT_WIDTH = ATT_HEADS * HEAD_DIM
KV_WIDTH = ATT_KV_HEADS * HEAD_DIM
HG_HEADS = 8
HG_WIDTH = HG_HEADS * HEAD_DIM
D_FF = 5632
GRID_W = 64
ROPE_THETA = 10000.0
NORM_EPS = 1e-6
HG_CHUNK = 128
HG_DIAG_LEVEL = 31
HG_UNROLL = 4
ATTN_KEY_CHUNK = 256
ATTN_HEADS_PER_PASS = 4

OFF_Q = 0
OFF_KV = ATT_WIDTH
OFF_QH = OFF_KV + 2 * KV_WIDTH
OFF_ZF = OFF_QH + HG_WIDTH
OFF_ZB = OFF_ZF + HG_WIDTH
OFF_VH = OFF_ZB + HG_WIDTH
OFF_OG = OFF_VH + HG_WIDTH
OFF_GA = OFF_OG + HG_WIDTH
OFF_GB = OFF_GA + D_MODEL

BF = jnp.bfloat16
F32 = jnp.float32

VMEM_LIMIT_BYTES = 56 * 1024 * 1024


def _params(*sem):
    return pltpu.CompilerParams(dimension_semantics=sem, vmem_limit_bytes=VMEM_LIMIT_BYTES)


def _rms(x):
    return x * lax.rsqrt(jnp.mean(x * x, axis=-1, keepdims=True) + NORM_EPS)


def _dot(a, b):
    return jnp.dot(a, b, preferred_element_type=F32)


def _dot_nt(a, b):
    return lax.dot_general(a, b, (((1,), (1,)), ((), ())), preferred_element_type=F32)


def _dot_tn(a, b):
    return lax.dot_general(a, b, (((0,), (0,)), ((), ())), preferred_element_type=F32)


def _ada_kernel(c_ref, w_ref, b_ref, o_ref):
    c = c_ref[...]
    s = (c * jax.nn.sigmoid(c)).astype(BF)
    o_ref[...] = _dot(s, w_ref[...].astype(BF)) + b_ref[...]


def _ada(cond, w, b):
    rows, d = cond.shape
    n = w.shape[1]
    tn = 1536
    return pl.pallas_call(
        _ada_kernel,
        grid=(n // tn,),
        in_specs=[
            pl.BlockSpec((rows, d), lambda j: (0, 0)),
            pl.BlockSpec((d, tn), lambda j: (0, j)),
            pl.BlockSpec((1, tn), lambda j: (0, j)),
        ],
        out_specs=pl.BlockSpec((rows, tn), lambda j: (0, j)),
        out_shape=jax.ShapeDtypeStruct((rows, n), F32),
        compiler_params=_params("arbitrary"),
        name="ada_mod",
    )(cond, w, b)


def _modnorm_kernel(x_ref, g_ref, sc_ref, sh_ref, o_ref):
    y = _rms(x_ref[...]) * g_ref[...]
    o_ref[...] = (y * (1.0 + sc_ref[0]) + sh_ref[0]).astype(BF)


def _modnorm(x, g, sc, sh, seq):
    m, d = x.shape
    tm = 512
    tpb = seq // tm
    return pl.pallas_call(
        _modnorm_kernel,
        grid=(m // tm,),
        in_specs=[
            pl.BlockSpec((tm, d), lambda i: (i, 0)),
            pl.BlockSpec((1, d), lambda i: (0, 0)),
            pl.BlockSpec((1, 1, d), lambda i: (i // tpb, 0, 0)),
            pl.BlockSpec((1, 1, d), lambda i: (i // tpb, 0, 0)),
        ],
        out_specs=pl.BlockSpec((tm, d), lambda i: (i, 0)),
        out_shape=jax.ShapeDtypeStruct((m, d), BF),
        compiler_params=_params("arbitrary"),
        name="modnorm",
    )(x, g, sc, sh)


def _rope(y, cos_ref, sin_ref):
    lane = lax.broadcasted_iota(jnp.int32, y.shape, 1)
    first = (lane & 32) == 0
    swapped = jnp.where(first, pltpu.roll(y, 96, 1), pltpu.roll(y, 32, 1))
    return y * cos_ref[...] + swapped * sin_ref[...]


def _store_heads(o_ref, h, y):
    nbb, _, ts, _ = o_ref.shape
    for bl in range(nbb):
        o_ref[bl, h] = y[bl * ts:(bl + 1) * ts].astype(o_ref.dtype)


def _q_epilogue(rope, acc, refs):
    if rope:
        qn_ref, cos_ref, sin_ref, o_ref = refs
    else:
        qn_ref, o_ref = refs
    scale = HEAD_DIM ** -0.5
    for h in range(ATT_HEADS):
        y = _rms(acc[:, h * HEAD_DIM:(h + 1) * HEAD_DIM]) * qn_ref[...]
        if rope:
            y = _rope(y, cos_ref, sin_ref)
        _store_heads(o_ref, h, y * scale)


def _kv_epilogue(rope, acc, refs):
    if rope:
        kn_ref, cos_ref, sin_ref, kb_ref, vb_ref = refs
    else:
        kn_ref, kb_ref, vb_ref, kf_ref, vf_ref = refs
    for h in range(ATT_KV_HEADS):
        sl = slice(h * HEAD_DIM, (h + 1) * HEAD_DIM)
        k = _rms(acc[:, sl]) * kn_ref[...]
        v = acc[:, KV_WIDTH + h * HEAD_DIM:KV_WIDTH + (h + 1) * HEAD_DIM]
        if rope:
            k = _rope(k, cos_ref, sin_ref)
        else:
            kf_ref[:, sl] = k
            vf_ref[:, sl] = v
        _store_heads(kb_ref, h, k)
        _store_heads(vb_ref, h, v)


def _heads_epilogue(act, acc, refs):
    (o_ref,) = refs
    for h in range(HG_HEADS):
        y = acc[:, h * HEAD_DIM:(h + 1) * HEAD_DIM]
        if act == "silu":
            y = y * jax.nn.sigmoid(y)
        _store_heads(o_ref, h, y)


def _forget_epilogue(acc, refs):
    lbp_ref, lf_ref, k_ref = refs
    p = lbp_ref[...]
    e = jnp.exp(p - jnp.max(p, axis=0, keepdims=True))
    lb = e[0:1] / jnp.sum(e, axis=0, keepdims=True)
    z = acc
    t = jnp.exp(-jnp.abs(z))
    r = 1.0 / (1.0 + t)
    tr = t * r
    pos = z >= 0
    sig_p = jnp.where(pos, r, tr)
    sig_n = jnp.where(pos, tr, r)
    logf = jnp.log(lb + (1.0 - lb) * sig_p)
    kk = (1.0 - lb) * sig_n
    for h in range(HG_HEADS):
        sl = slice(h * HEAD_DIM, (h + 1) * HEAD_DIM)
        _store_heads(lf_ref, h, logf[:, sl])
        _store_heads(k_ref, h, kk[:, sl])


def _sigmoid_epilogue(acc, refs):
    (o_ref,) = refs
    o_ref[...] = jax.nn.sigmoid(acc).astype(BF)


def _proj_kernel(epilogue, h_ref, w_ref, *refs):
    epilogue(_dot(h_ref[...], w_ref[...]), refs)


def _proj(h, w, tn, epilogue, extra, extra_specs, out_shapes, out_specs, tm, name):
    m, d = h.shape
    ncols = w.shape[1]
    return pl.pallas_call(
        functools.partial(_proj_kernel, epilogue),
        grid=(m // tm, ncols // tn),
        in_specs=[
            pl.BlockSpec((tm, d), lambda i, j: (i, 0)),
            pl.BlockSpec((d, tn), lambda i, j: (0, j)),
        ] + extra_specs,
        out_specs=out_specs,
        out_shape=out_shapes,
        compiler_params=_params("arbitrary", "arbitrary"),
        name=name,
    )(h, w, *extra)


PROJ_TOKENS = 1024


def _in_projection(h, ws, batch, seq, rope_tabs, q_norm, k_norm, lb_f, lb_b, tag):
    m = h.shape[0]
    tm = PROJ_TOKENS
    rope = rope_tabs is not None
    if seq >= tm:
        tpb = seq // tm
        head_block = (1, tm)

        def head_index(i, j):
            return (i // tpb, 0, i % tpb, 0)
    else:
        tpb = 1
        head_block = (tm // seq, seq)

        def head_index(i, j):
            return (i, 0, 0, 0)

    def head_spec(nh):
        return pl.BlockSpec((head_block[0], nh, head_block[1], HEAD_DIM), head_index)

    def head_shape(nh, dt=BF):
        return jax.ShapeDtypeStruct((batch, nh, seq, HEAD_DIM), dt)

    vec_spec = pl.BlockSpec((1, HEAD_DIM), lambda i, j: (0, 0))
    tab_spec = pl.BlockSpec((tm, HEAD_DIM), lambda i, j: (i % tpb, 0))
    rope_in = list(rope_tabs) if rope else []
    rope_specs = [tab_spec, tab_spec] if rope else []

    q = _proj(h, ws["q"], ATT_WIDTH, functools.partial(_q_epilogue, rope),
              [q_norm] + rope_in, [vec_spec] + rope_specs,
              head_shape(ATT_HEADS), head_spec(ATT_HEADS), tm, "proj_q_" + tag)

    kv_shapes = [head_shape(ATT_KV_HEADS), head_shape(ATT_KV_HEADS)]
    kv_specs = [head_spec(ATT_KV_HEADS), head_spec(ATT_KV_HEADS)]
    if not rope:
        tok_spec = pl.BlockSpec((tm, KV_WIDTH), lambda i, j: (i, 0))
        kv_shapes += [jax.ShapeDtypeStruct((m, KV_WIDTH), F32)] * 2
        kv_specs += [tok_spec, tok_spec]
    kv = _proj(h, ws["kv"], 2 * KV_WIDTH, functools.partial(_kv_epilogue, rope),
               [k_norm] + rope_in, [vec_spec] + rope_specs,
               kv_shapes, kv_specs, tm, "proj_kv_" + tag)

    def heads(seg, act):
        return _proj(h, ws[seg], HG_WIDTH, functools.partial(_heads_epilogue, act),
                     [], [], head_shape(HG_HEADS), head_spec(HG_HEADS), tm, "proj_" + seg + "_" + tag)

    qh = heads("qh", "silu")
    vh = heads("vh", "none")
    og = heads("og", "silu")

    def forget(seg, lbp):
        return _proj(h, ws[seg], HG_WIDTH, _forget_epilogue,
                     [lbp], [pl.BlockSpec(lbp.shape, lambda i, j: (0, 0))],
                     [head_shape(HG_HEADS, F32), head_shape(HG_HEADS)],
                     [head_spec(HG_HEADS), head_spec(HG_HEADS)], tm, "proj_" + seg + "_" + tag)

    lf_f, k_f = forget("zf", lb_f)
    lf_b, k_b = forget("zb", lb_b)

    tn = 1024
    gates = _proj(h, ws["gates"], tn, _sigmoid_epilogue, [], [],
                  jax.ShapeDtypeStruct((m, 2 * D_MODEL), BF),
                  pl.BlockSpec((tm, tn), lambda i, j: (i, j)), tm, "proj_gates_" + tag)
    return q, kv, qh, vh, og, lf_f, k_f, lf_b, k_b, gates


def _attn_kernel(q_ref, k_ref, v_ref, o_ref, vt_scr):
    @pl.when(pl.program_id(2) == 0)
    def _():
        vt_scr[...] = v_ref[0, 0].T

    tq = q_ref.shape[2]
    keys = k_ref.shape[2]
    n_chunks = keys // ATTN_KEY_CHUNK
    n_pass = GQA_GROUP // ATTN_HEADS_PER_PASS
    qs = [jnp.concatenate([q_ref[0, g] for g in range(ps * ATTN_HEADS_PER_PASS,
                                                       (ps + 1) * ATTN_HEADS_PER_PASS)], axis=0)
          for ps in range(n_pass)]
    items = [(c, ps) for c in range(n_chunks) for ps in range(n_pass)]

    def scores(item):
        c, ps = item
        return _dot_nt(k_ref[0, 0, c * ATTN_KEY_CHUNK:(c + 1) * ATTN_KEY_CHUNK, :], qs[ps])

    state = [None] * n_pass
    s_next = scores(items[0])
    for i, (c, ps) in enumerate(items):
        ks = slice(c * ATTN_KEY_CHUNK, (c + 1) * ATTN_KEY_CHUNK)
        s = s_next
        if i + 1 < len(items):
            s_next = scores(items[i + 1])
        smax = jnp.max(s, axis=0, keepdims=True)
        if c == 0:
            p = jnp.exp(s - smax)
            state[ps] = (smax, jnp.sum(p, axis=0, keepdims=True), _dot(vt_scr[:, ks], p.astype(BF)))
        else:
            m, den, acc = state[ps]
            m_new = jnp.maximum(m, smax)
            alpha = jnp.exp(m - m_new)
            p = jnp.exp(s - m_new)
            state[ps] = (m_new, den * alpha + jnp.sum(p, axis=0, keepdims=True),
                         acc * alpha + _dot(vt_scr[:, ks], p.astype(BF)))
    for ps in range(n_pass):
        _, den, acc = state[ps]
        o = acc / den
        for i in range(ATTN_HEADS_PER_PASS):
            g = ps * ATTN_HEADS_PER_PASS + i
            o_ref[0, :, g * HEAD_DIM:(g + 1) * HEAD_DIM] = o[:, i * tq:(i + 1) * tq].T.astype(BF)


def _attention(q, k, v, tq, name):
    batch, _, seq, _ = q.shape
    keys = k.shape[2]
    kv_spec = pl.BlockSpec((1, 1, keys, HEAD_DIM), lambda b, kv, t: (b, kv, 0, 0))
    return pl.pallas_call(
        _attn_kernel,
        grid=(batch, ATT_KV_HEADS, seq // tq),
        in_specs=[pl.BlockSpec((1, GQA_GROUP, tq, HEAD_DIM), lambda b, kv, t: (b, kv, t, 0)),
                  kv_spec, kv_spec],
        out_specs=pl.BlockSpec((1, tq, GQA_GROUP * HEAD_DIM), lambda b, kv, t: (b, t, kv)),
        out_shape=jax.ShapeDtypeStruct((batch, seq, ATT_WIDTH), BF),
        scratch_shapes=[pltpu.VMEM((HEAD_DIM, keys), BF)],
        compiler_params=_params("arbitrary", "arbitrary", "arbitrary"),
        name=name,
    )(q, k, v)


def _split3(x):
    hi = x.astype(BF)
    r1 = x - hi.astype(F32)
    mid = r1.astype(BF)
    lo = (r1 - mid.astype(F32)).astype(BF)
    return hi, mid, lo


def _block_reference(b3, hb, rev):
    n8 = b3.shape[0]
    if hb < 8:
        def row(r):
            return jnp.broadcast_to(b3[:, r:r + 1, :], b3.shape)
        if hb == 4:
            return row(4 if rev else 3)
        assert hb == 2
        sub = lax.broadcasted_iota(jnp.int32, b3.shape, 1)
        r0, r1 = (2, 6) if rev else (1, 5)
        return jnp.where(sub < 4, row(r0), row(r1))
    g = hb // 8
    nblk = n8 // (2 * g)
    edge = (b3[:, 0:1, :] if rev else b3[:, 7:8, :]).reshape(nblk, 2 * g, 1, b3.shape[2])
    pick = edge[:, g:g + 1] if rev else edge[:, g - 1:g]
    return jnp.broadcast_to(pick, (nblk, 2 * g, 8, b3.shape[2])).reshape(b3.shape)


def _chunk_cumsum(lf, tri_b):
    hi, mid, lo = _split3(lf)
    return _dot(tri_b, hi) + _dot(tri_b, mid) + _dot(tri_b, lo)


def _hg_chunk(q, k, lf, b, vt, st, level, rev):
    c = q.shape[0]
    n8 = c // 8
    b3 = b.reshape(n8, 8, HEAD_DIM)
    q32 = q.astype(F32)
    k32 = k.astype(F32)
    sub = lax.broadcasted_iota(jnp.int32, (c, HEAD_DIM), 0)
    qbit = 0 if rev else 1

    def rows8(x, r):
        return x[r * 8:(r + 1) * 8]

    diag = _dot_nt(q, k)
    a_rows = [jnp.where(rows8(level, r) == HG_DIAG_LEVEL, rows8(diag, r), 0.0) for r in range(n8)]
    hb, j = 1, 0
    while hb < c:
        is_query = ((sub & hb) == 0) if rev else ((sub & hb) != 0)
        if hb == 1:
            arg = jnp.where(is_query, lf, 0.0)
        else:
            arg = -jnp.abs(b - _block_reference(b3, hb, rev).reshape(c, HEAD_DIM))
        e = jnp.exp(arg)
        if hb < 8:
            x = (jnp.where(is_query, q32, k32) * e).astype(BF)
            prod = _dot_nt(x, x)
            q_groups = list(range(n8))
        else:
            q_groups = [r for r in range(n8) if ((r * 8 // hb) & 1) == qbit]
            mixed = jnp.concatenate([rows8(q32 if r in q_groups else k32, r) for r in range(n8)], axis=0)
            y = mixed * e
            lhs = jnp.concatenate([rows8(y, r) for r in q_groups], axis=0).astype(BF)
            prod = _dot_nt(lhs, y.astype(BF))
        for i, r in enumerate(q_groups):
            a_rows[r] = jnp.where(rows8(level, r) == j, rows8(prod, i), a_rows[r])
        hb, j = 2 * hb, j + 1
    a = jnp.concatenate(a_rows, axis=0)
    btot = b[0:1, :] if rev else b[c - 1:c, :]
    qs = (q32 * jnp.exp(b)).astype(BF)
    ks = (k32 * jnp.exp(btot - b)).astype(BF)
    o = _dot_nt(jnp.concatenate([a.astype(BF), qs], axis=1),
                jnp.concatenate([vt, st.astype(BF)], axis=1))
    st_new = st * jnp.exp(btot) + _dot(vt, ks)
    return o, st_new


def _pair_levels(c, rev):
    t = lax.broadcasted_iota(jnp.int32, (c, c), 0)
    s = lax.broadcasted_iota(jnp.int32, (c, c), 1)
    x = t ^ s
    lvl = jnp.zeros((c, c), jnp.int32)
    p = 2
    while p < c:
        lvl = lvl + jnp.where(x >= p, 1, 0)
        p *= 2
    valid = (s > t) if rev else (s < t)
    return jnp.where(valid, lvl, jnp.where(s == t, HG_DIAG_LEVEL, -1))


def _hgrn_kernel(seq, heads, zero_init, q_ref, kf_ref, kb_ref, lf_ref, lb_ref, v_ref, og_ref, *refs):
    if zero_init:
        hn_ref, o_ref, sf_ref, sb_ref, vt_scr, bf_scr, bb_scr, of_scr, ob_scr = refs
    else:
        s0f_ref, s0b_ref, hn_ref, o_ref, vt_scr, bf_scr, bb_scr, of_scr, ob_scr = refs
    c = HG_CHUNK
    n = seq // c
    t = lax.broadcasted_iota(jnp.int32, (c, c), 0)
    s = lax.broadcasted_iota(jnp.int32, (c, c), 1)
    tril_b = jnp.where(s <= t, 1.0, 0.0).astype(BF)
    triu_b = jnp.where(s >= t, 1.0, 0.0).astype(BF)
    level_f = _pair_levels(c, False)
    level_b = _pair_levels(c, True)

    def rows(i):
        return pl.ds(pl.multiple_of(i * c, c), c)

    def head_body(h, _):
        def prepare(i, _):
            r = rows(i)
            vt_scr[i] = v_ref[0, h, r, :].T
            bf_scr[r, :] = _chunk_cumsum(lf_ref[0, h, r, :], tril_b)
            bb_scr[r, :] = _chunk_cumsum(lb_ref[0, h, r, :], triu_b)
            return 0

        lax.fori_loop(0, n, prepare, 0, unroll=2)

        def body(i, carry):
            st_f, st_b = carry
            rf = rows(i)
            rb = rows(n - 1 - i)
            o_f, st_f = _hg_chunk(q_ref[0, h, rf, :], kf_ref[0, h, rf, :], lf_ref[0, h, rf, :],
                                  bf_scr[rf, :], vt_scr[i], st_f, level_f, False)
            of_scr[rf, :] = o_f
            o_b, st_b = _hg_chunk(q_ref[0, h, rb, :], kb_ref[0, h, rb, :], lb_ref[0, h, rb, :],
                                  bb_scr[rb, :], vt_scr[n - 1 - i], st_b, level_b, True)
            ob_scr[rb, :] = o_b
            return st_f, st_b

        if zero_init:
            st0 = (jnp.zeros((HEAD_DIM, HEAD_DIM), F32), jnp.zeros((HEAD_DIM, HEAD_DIM), F32))
        else:
            st0 = (s0f_ref[0, h].T, s0b_ref[0, h].T)
        st_f, st_b = lax.fori_loop(0, n, body, st0, unroll=min(n, HG_UNROLL))
        o = of_scr[...] + ob_scr[...]
        y = _rms(o) * hn_ref[...] * og_ref[0, h].astype(F32)
        o_ref[0, h] = y.astype(BF)
        if zero_init:
            sf_ref[0, h] = st_f.T
            sb_ref[0, h] = st_b.T
        return 0

    lax.fori_loop(0, heads, head_body, 0)


def _hgrn(qh, k_f, k_b, lf_f, lf_b, vh, og, hg_norm, s0f, s0b, heads_per_step, name):
    batch, heads, seq, hd = qh.shape
    hps = heads_per_step
    zero_init = s0f is None
    head_spec = pl.BlockSpec((1, hps, seq, hd), lambda b, h: (b, h, 0, 0))
    state_spec = pl.BlockSpec((1, hps, hd, hd), lambda b, h: (b, h, 0, 0))
    vec_spec = pl.BlockSpec((1, hd), lambda b, h: (0, 0))
    o_shape = jax.ShapeDtypeStruct((batch, heads, seq, hd), BF)
    ins = [qh, k_f, k_b, lf_f, lf_b, vh, og]
    in_specs = [head_spec] * 7
    if zero_init:
        state_shape = jax.ShapeDtypeStruct((batch, heads, hd, hd), F32)
        out_shape = [o_shape, state_shape, state_shape]
        out_specs = [head_spec, state_spec, state_spec]
    else:
        ins += [s0f, s0b]
        in_specs += [state_spec, state_spec]
        out_shape = o_shape
        out_specs = head_spec
    ins.append(hg_norm)
    in_specs.append(vec_spec)
    return pl.pallas_call(
        functools.partial(_hgrn_kernel, seq, hps, zero_init),
        grid=(batch, heads // hps),
        in_specs=in_specs,
        out_specs=out_specs,
        out_shape=out_shape,
        scratch_shapes=[pltpu.VMEM((seq // HG_CHUNK, hd, HG_CHUNK), BF)]
        + [pltpu.VMEM((seq, hd), F32)] * 4,
        compiler_params=_params("arbitrary", "arbitrary"),
        name=name,
    )(*ins)


def _merge_kernel(att_ref, hg_ref, ga_ref, gb_ref, x_ref, wa_ref, wh_ref, wo_ref,
                  g1_ref, sc2_ref, sh2_ref, npost_ref, npre_ref, x1_ref, h2_ref):
    a = _dot(att_ref[...], wa_ref[...])
    hg = jnp.concatenate([hg_ref[0, h] for h in range(HG_HEADS)], axis=1)
    b = _dot(hg, wh_ref[...])
    m = (ga_ref[...].astype(F32) * a + gb_ref[...].astype(F32) * b).astype(BF)
    mo = _dot(m, wo_ref[...])
    x1 = x_ref[...] + g1_ref[0] * (_rms(mo) * npost_ref[...])
    x1_ref[...] = x1
    h2 = _rms(x1) * npre_ref[...]
    h2_ref[...] = (h2 * (1.0 + sc2_ref[0]) + sh2_ref[0]).astype(BF)


def _merge(att, hg, gates, x, wa, wh, wo, g1, sc2, sh2, npost, npre, seq, name):
    m, d = x.shape
    tm = 256
    tpb = seq // tm
    w_att = att.shape[1]
    hg_tpb = hg.shape[2] // tm
    hg_spec = pl.BlockSpec((1, HG_HEADS, tm, HEAD_DIM), lambda i: (i // hg_tpb, 0, i % hg_tpb, 0))

    def const(shape):
        return pl.BlockSpec(shape, lambda i: (0, 0), pipeline_mode=pl.Buffered(1))

    mod_spec = pl.BlockSpec((1, 1, d), lambda i: (i // tpb, 0, 0))
    vec_spec = pl.BlockSpec((1, d), lambda i: (0, 0))
    return pl.pallas_call(
        _merge_kernel,
        grid=(m // tm,),
        in_specs=[
            pl.BlockSpec((tm, w_att), lambda i: (i, 0)),
            hg_spec,
            pl.BlockSpec((tm, d), lambda i: (i, 0)),
            pl.BlockSpec((tm, d), lambda i: (i, 1)),
            pl.BlockSpec((tm, d), lambda i: (i, 0)),
            const(wa.shape), const(wh.shape), const(wo.shape),
            mod_spec, mod_spec, mod_spec, vec_spec, vec_spec,
        ],
        out_specs=[pl.BlockSpec((tm, d), lambda i: (i, 0)), pl.BlockSpec((tm, d), lambda i: (i, 0))],
        out_shape=[jax.ShapeDtypeStruct((m, d), F32), jax.ShapeDtypeStruct((m, d), BF)],
        compiler_params=_params("arbitrary"),
        name=name,
    )(att, hg, gates, gates, x, wa, wh, wo, g1, sc2, sh2, npost, npre)


def _ffn_up_kernel(h_ref, wg_ref, wu_ref, a_ref):
    h = h_ref[...]
    g = _dot(h, wg_ref[...])
    u = _dot(h, wu_ref[...])
    a_ref[...] = (g * jax.nn.sigmoid(g) * u).astype(BF)


def _ffn_down_kernel(a_ref, wo_ref, x1_ref, g2_ref, npost_ref, o_ref):
    y = _rms(_dot(a_ref[...], wo_ref[...])) * npost_ref[...]
    o_ref[...] = x1_ref[...] + g2_ref[0] * y


def _ffn(h2, x1, w_in, w_out, g2, npost, seq, name):
    m, d = x1.shape
    tm, tf = PROJ_TOKENS, 512
    nf = D_FF // tf
    act = pl.pallas_call(
        _ffn_up_kernel,
        grid=(m // tm, nf),
        in_specs=[
            pl.BlockSpec((tm, d), lambda i, f: (i, 0)),
            pl.BlockSpec((d, tf), lambda i, f: (0, f)),
            pl.BlockSpec((d, tf), lambda i, f: (0, nf + f)),
        ],
        out_specs=pl.BlockSpec((tm, tf), lambda i, f: (i, f)),
        out_shape=jax.ShapeDtypeStruct((m, D_FF), BF),
        compiler_params=_params("arbitrary", "arbitrary"),
        name=name + "_up",
    )(h2, w_in, w_in)
    td = 256
    tpb = seq // td
    return pl.pallas_call(
        _ffn_down_kernel,
        grid=(m // td,),
        in_specs=[
            pl.BlockSpec((td, D_FF), lambda i: (i, 0)),
            pl.BlockSpec((D_FF, d), lambda i: (0, 0), pipeline_mode=pl.Buffered(1)),
            pl.BlockSpec((td, d), lambda i: (i, 0)),
            pl.BlockSpec((1, 1, d), lambda i: (i // tpb, 0, 0)),
            pl.BlockSpec((1, d), lambda i: (0, 0)),
        ],
        out_specs=pl.BlockSpec((td, d), lambda i: (i, 0)),
        out_shape=jax.ShapeDtypeStruct((m, d), F32),
        compiler_params=_params("arbitrary"),
        name=name + "_down",
    )(act, w_out, x1, g2, npost)


def _rope_tables(n_tokens):
    rows = n_tokens // GRID_W
    half = HEAD_DIM // 4
    r = jnp.repeat(jnp.arange(rows), GRID_W).astype(F32)
    col = jnp.tile(jnp.arange(GRID_W), rows).astype(F32)
    inv = ROPE_THETA ** (-jnp.arange(half, dtype=F32) / half)
    ar = r[:, None] * inv
    ac = col[:, None] * inv
    cos = jnp.concatenate([jnp.cos(ar), jnp.cos(ar), jnp.cos(ac), jnp.cos(ac)], axis=-1)
    sin = jnp.concatenate([-jnp.sin(ar), jnp.sin(ar), -jnp.sin(ac), jnp.sin(ac)], axis=-1)
    return cos, sin


def _layer(x3, mods, weights, rope_tabs, cache, states, tag):
    batch, seq, d = x3.shape
    m = batch * seq
    x = x3.reshape(m, d)
    (w_in, q_norm, k_norm, lb_f, lb_b, hg_norm, wa, wh, wo, w_ffn_in, w_ffn_out,
     n_pre_mix, n_post_mix, n_pre_ffn, n_post_ffn) = weights
    nb = mods.shape[0]
    sh1, sc1, g1, sh2, sc2, g2 = [mods[:, i].reshape(nb, 1, d) for i in range(6)]
    seq_mod = seq if nb == batch else m

    h = _modnorm(x, n_pre_mix, sc1, sh1, seq_mod)
    q, kv, qh, vh, og, lf_f, k_f, lf_b, k_b, gates = _in_projection(
        h, w_in, batch, seq, rope_tabs, q_norm, k_norm, lb_f, lb_b, tag)

    if cache is None:
        k_bf, v_bf, k_f32, v_f32 = kv
        att = _attention(q, k_bf, v_bf, seq, "attn_" + tag)
    else:
        k_bf, v_bf = kv
        att = _attention(q, jnp.concatenate([k_bf, cache[0]], axis=2),
                         jnp.concatenate([v_bf, cache[1]], axis=2), 256, "attn_" + tag)
        k_f32 = v_f32 = None

    hps = HG_HEADS if seq <= 2 * HG_CHUNK else 1
    if states is None:
        hg, s_f, s_b = _hgrn(qh, k_f, k_b, lf_f, lf_b, vh, og, hg_norm, None, None, hps, "hgrn_" + tag)
    else:
        hg = _hgrn(qh, k_f, k_b, lf_f, lf_b, vh, og, hg_norm, states[0], states[1], hps, "hgrn_" + tag)
        s_f = s_b = None

    x1, h2 = _merge(att.reshape(m, ATT_WIDTH), hg, gates, x, wa, wh, wo,
                    g1, sc2, sh2, n_post_mix, n_pre_ffn, seq_mod, "merge_" + tag)
    y = _ffn(h2, x1, w_ffn_in, w_ffn_out, g2, n_post_ffn, seq_mod, "ffn_" + tag)
    return y.reshape(batch, seq, d), k_f32, v_f32, s_f, s_b


def kernel(x_prompt, x_sample, cache_k, cache_v, state_fwd, state_bwd, c, c_ctx, w_ada, b_ada,
           norm_pre_mix, norm_post_mix, norm_pre_ffn, norm_post_ffn, w_in, q_norm, k_norm,
           lb_fwd, lb_bwd, hg_norm, w_br_att, w_br_hg, w_out, w_ffn_in, w_ffn_out):
    depth = w_in.shape[0]
    assert depth == 1 and lb_fwd.shape[0] == 2
    batch, seq, d = x_prompt.shape
    dec_batch, dec_seq, _ = x_sample.shape
    past = cache_k.shape[2]

    cond = jnp.concatenate([c_ctx[None, :], c, jnp.zeros((8 - 1 - dec_batch, d), F32)], axis=0)
    mods = _ada(cond, w_ada[0], b_ada[0][None, :]).reshape(8, 6, d)

    segments = (("q", OFF_Q, OFF_KV), ("kv", OFF_KV, OFF_QH), ("qh", OFF_QH, OFF_ZF),
                ("zf", OFF_ZF, OFF_ZB), ("zb", OFF_ZB, OFF_VH), ("vh", OFF_VH, OFF_OG),
                ("og", OFF_OG, OFF_GA), ("gates", OFF_GA, OFF_GB + D_MODEL))
    w_in_segments = {name: w_in[0, :, a:b].astype(BF) for name, a, b in segments}
    weights = (
        w_in_segments, q_norm, k_norm, lb_fwd, lb_bwd, hg_norm,
        w_br_att[0].astype(BF), w_br_hg[0].astype(BF), w_out[0].astype(BF),
        w_ffn_in[0].astype(BF), w_ffn_out[0].astype(BF),
        norm_pre_mix, norm_post_mix, norm_pre_ffn, norm_post_ffn,
    )

    y_p, k_c, v_c, s_f, s_b = _layer(x_prompt, mods[0:1], weights, None, None, None, "ctx")

    cache = (
        cache_k[:, 0].transpose(0, 2, 1, 3).astype(BF),
        cache_v[:, 0].transpose(0, 2, 1, 3).astype(BF),
    )
    states = (state_fwd[:, 0], state_bwd[:, 0])
    y_s, _, _, _, _ = _layer(x_sample, mods[1:1 + dec_batch], weights, _rope_tables(dec_seq),
                             cache, states, "lat")

    new_k = k_c.reshape(batch, 1, seq, ATT_KV_HEADS, HEAD_DIM)
    new_v = v_c.reshape(batch, 1, seq, ATT_KV_HEADS, HEAD_DIM)
    return (y_p, y_s, new_k, new_v, s_f[:, None], s_b[:, None])
```

```python
import functools

import jax
import jax.numpy as jnp
from jax import lax
from jax.experimental import pallas as pl
from jax.experimental.pallas import tpu as pltpu

D_MODEL = 2048
HEAD_DIM = 128
ATT_HEADS = 8
ATT_KV_HEADS = 2
GQA_GROUP = ATT_HEADS // ATT_KV_HEADS
ATT_WIDTH = ATT_HEADS * HEAD_DIM
KV_WIDTH = ATT_KV_HEADS * HEAD_DIM
HG_HEADS = 8
HG_WIDTH = HG_HEADS * HEAD_DIM
D_FF = 5632
GRID_W = 64
ROPE_THETA = 10000.0
NORM_EPS = 1e-6
HG_CHUNK = 128
HG_DIAG_LEVEL = 31
HG_UNROLL = 4
ATTN_KEY_CHUNK = 256
ATTN_HEADS_PER_PASS = 4
ATTN_ONES_ROWS = 16
LOG2E = 1.4426950408889634

OFF_Q = 0
OFF_KV = ATT_WIDTH
OFF_QH = OFF_KV + 2 * KV_WIDTH
OFF_ZF = OFF_QH + HG_WIDTH
OFF_ZB = OFF_ZF + HG_WIDTH
OFF_VH = OFF_ZB + HG_WIDTH
OFF_OG = OFF_VH + HG_WIDTH
OFF_GA = OFF_OG + HG_WIDTH
OFF_GB = OFF_GA + D_MODEL

BF = jnp.bfloat16
F32 = jnp.float32

VMEM_LIMIT_BYTES = 56 * 1024 * 1024


def _params(*sem):
    return pltpu.CompilerParams(dimension_semantics=sem, vmem_limit_bytes=VMEM_LIMIT_BYTES)


def _rms(x):
    return x * lax.rsqrt(jnp.mean(x * x, axis=-1, keepdims=True) + NORM_EPS)


def _dot(a, b):
    return jnp.dot(a, b, preferred_element_type=F32)


def _dot_nt(a, b):
    return lax.dot_general(a, b, (((1,), (1,)), ((), ())), preferred_element_type=F32)


def _dot_tn(a, b):
    return lax.dot_general(a, b, (((0,), (0,)), ((), ())), preferred_element_type=F32)


def _ada_kernel(c_ref, w_ref, b_ref, o_ref):
    c = c_ref[...]
    s = (c * jax.nn.sigmoid(c)).astype(BF)
    o_ref[...] = _dot(s, w_ref[...].astype(BF)) + b_ref[...]


def _ada(cond, w, b):
    rows, d = cond.shape
    n = w.shape[1]
    tn = 1536
    return pl.pallas_call(
        _ada_kernel,
        grid=(n // tn,),
        in_specs=[
            pl.BlockSpec((rows, d), lambda j: (0, 0)),
            pl.BlockSpec((d, tn), lambda j: (0, j)),
            pl.BlockSpec((1, tn), lambda j: (0, j)),
        ],
        out_specs=pl.BlockSpec((rows, tn), lambda j: (0, j)),
        out_shape=jax.ShapeDtypeStruct((rows, n), F32),
        compiler_params=_params("arbitrary"),
        name="ada_mod",
    )(cond, w, b)


def _modnorm_kernel(x_ref, g_ref, sc_ref, sh_ref, o_ref):
    y = _rms(x_ref[...]) * g_ref[...]
    o_ref[...] = (y * (1.0 + sc_ref[0]) + sh_ref[0]).astype(BF)


def _modnorm(x, g, sc, sh, seq):
    m, d = x.shape
    tm = 512
    tpb = seq // tm
    return pl.pallas_call(
        _modnorm_kernel,
        grid=(m // tm,),
        in_specs=[
            pl.BlockSpec((tm, d), lambda i: (i, 0)),
            pl.BlockSpec((1, d), lambda i: (0, 0)),
            pl.BlockSpec((1, 1, d), lambda i: (i // tpb, 0, 0)),
            pl.BlockSpec((1, 1, d), lambda i: (i // tpb, 0, 0)),
        ],
        out_specs=pl.BlockSpec((tm, d), lambda i: (i, 0)),
        out_shape=jax.ShapeDtypeStruct((m, d), BF),
        compiler_params=_params("arbitrary"),
        name="modnorm",
    )(x, g, sc, sh)


def _rope(y, cos_ref, sin_ref):
    lane = lax.broadcasted_iota(jnp.int32, y.shape, 1)
    first = (lane & 32) == 0
    swapped = jnp.where(first, pltpu.roll(y, 96, 1), pltpu.roll(y, 32, 1))
    return y * cos_ref[...] + swapped * sin_ref[...]


def _store_heads(o_ref, h, y):
    nbb, _, ts, _ = o_ref.shape
    for bl in range(nbb):
        o_ref[bl, h] = y[bl * ts:(bl + 1) * ts].astype(o_ref.dtype)


def _q_epilogue(rope, acc, col0, refs):
    if rope:
        qn_ref, cos_ref, sin_ref, o_ref = refs
    else:
        qn_ref, o_ref = refs
    scale = LOG2E * HEAD_DIM ** -0.5
    for hl in range(acc.shape[1] // HEAD_DIM):
        y = _rms(acc[:, hl * HEAD_DIM:(hl + 1) * HEAD_DIM]) * qn_ref[...]
        if rope:
            y = _rope(y, cos_ref, sin_ref)
        _store_heads(o_ref, col0 // HEAD_DIM + hl, y * scale)


def _kv_epilogue(rope, acc, col0, refs):
    if rope:
        kn_ref, cos_ref, sin_ref, kb_ref, vb_ref = refs
    else:
        kn_ref, kb_ref, vb_ref, kf_ref, vf_ref = refs
    is_k = col0 < KV_WIDTH
    for hl in range(acc.shape[1] // HEAD_DIM):
        h = (col0 % KV_WIDTH) // HEAD_DIM + hl
        sl = slice(h * HEAD_DIM, (h + 1) * HEAD_DIM)
        y = acc[:, hl * HEAD_DIM:(hl + 1) * HEAD_DIM]
        if is_k:
            y = _rms(y) * kn_ref[...]
        if not rope:
            (kf_ref if is_k else vf_ref)[:, sl] = y
        elif is_k:
            y = _rope(y, cos_ref, sin_ref)
        _store_heads(kb_ref if is_k else vb_ref, h, y)


def _heads_epilogue(act, acc, col0, refs):
    (o_ref,) = refs
    for hl in range(acc.shape[1] // HEAD_DIM):
        y = acc[:, hl * HEAD_DIM:(hl + 1) * HEAD_DIM]
        if act == "silu":
            y = y * jax.nn.sigmoid(y)
        _store_heads(o_ref, col0 // HEAD_DIM + hl, y)


def _forget_epilogue(acc, col0, refs):
    lbp_ref, lf_ref, k_ref = refs
    p = lbp_ref[:, col0:col0 + acc.shape[1]]
    e = jnp.exp(p - jnp.max(p, axis=0, keepdims=True))
    lb = e[0:1] / jnp.sum(e, axis=0, keepdims=True)
    z = acc
    t = jnp.exp(-jnp.abs(z))
    r = 1.0 / (1.0 + t)
    tr = t * r
    pos = z >= 0
    sig_p = jnp.where(pos, r, tr)
    sig_n = jnp.where(pos, tr, r)
    logf = jnp.log2(lb + (1.0 - lb) * sig_p)
    kk = (1.0 - lb) * sig_n
    for hl in range(acc.shape[1] // HEAD_DIM):
        sl = slice(hl * HEAD_DIM, (hl + 1) * HEAD_DIM)
        _store_heads(lf_ref, col0 // HEAD_DIM + hl, logf[:, sl])
        _store_heads(k_ref, col0 // HEAD_DIM + hl, kk[:, sl])


def _sigmoid_epilogue(acc, col0, refs):
    (o_ref,) = refs
    o_ref[:, col0:col0 + acc.shape[1]] = jax.nn.sigmoid(acc).astype(BF)


def _proj_kernel(epilogue, h_ref, w_ref, *refs):
    h = h_ref[...]
    starts = list(range(0, w_ref.shape[1], PROJ_COL_GROUP))

    def product(c0):
        return _dot(h, w_ref[:, c0:c0 + PROJ_COL_GROUP])

    nxt = product(starts[0])
    for i, c0 in enumerate(starts):
        acc = nxt
        if i + 1 < len(starts):
            nxt = product(starts[i + 1])
        epilogue(acc, c0, refs)


def _proj(h, w, tn, epilogue, extra, extra_specs, out_shapes, out_specs, tm, name):
    m, d = h.shape
    ncols = w.shape[1]
    return pl.pallas_call(
        functools.partial(_proj_kernel, epilogue),
        grid=(m // tm, ncols // tn),
        in_specs=[
            pl.BlockSpec((tm, d), lambda i, j: (i, 0)),
            pl.BlockSpec((d, tn), lambda i, j: (0, j)),
        ] + extra_specs,
        out_specs=out_specs,
        out_shape=out_shapes,
        compiler_params=_params("arbitrary", "arbitrary"),
        name=name,
    )(h, w, *extra)


PROJ_TOKENS = 1024
PROJ_COL_GROUP = 256


def _in_projection(h, ws, batch, seq, rope_tabs, q_norm, k_norm, lb_f, lb_b, tag):
    m = h.shape[0]
    tm = PROJ_TOKENS
    rope = rope_tabs is not None
    if seq >= tm:
        tpb = seq // tm
        head_block = (1, tm)

        def head_index(i, j):
            return (i // tpb, 0, i % tpb, 0)
    else:
        tpb = 1
        head_block = (tm // seq, seq)

        def head_index(i, j):
            return (i, 0, 0, 0)

    def head_spec(nh):
        return pl.BlockSpec((head_block[0], nh, head_block[1], HEAD_DIM), head_index)

    def head_shape(nh, dt=BF):
        return jax.ShapeDtypeStruct((batch, nh, seq, HEAD_DIM), dt)

    vec_spec = pl.BlockSpec((1, HEAD_DIM), lambda i, j: (0, 0))
    tab_spec = pl.BlockSpec((tm, HEAD_DIM), lambda i, j: (i % tpb, 0))
    rope_in = list(rope_tabs) if rope else []
    rope_specs = [tab_spec, tab_spec] if rope else []

    q = _proj(h, ws["q"], ATT_WIDTH, functools.partial(_q_epilogue, rope),
              [q_norm] + rope_in, [vec_spec] + rope_specs,
              head_shape(ATT_HEADS), head_spec(ATT_HEADS), tm, "proj_q_" + tag)

    kv_shapes = [head_shape(ATT_KV_HEADS), head_shape(ATT_KV_HEADS)]
    kv_specs = [head_spec(ATT_KV_HEADS), head_spec(ATT_KV_HEADS)]
    if not rope:
        tok_spec = pl.BlockSpec((tm, KV_WIDTH), lambda i, j: (i, 0))
        kv_shapes += [jax.ShapeDtypeStruct((m, KV_WIDTH), F32)] * 2
        kv_specs += [tok_spec, tok_spec]
    kv = _proj(h, ws["kv"], 2 * KV_WIDTH, functools.partial(_kv_epilogue, rope),
               [k_norm] + rope_in, [vec_spec] + rope_specs,
               kv_shapes, kv_specs, tm, "proj_kv_" + tag)

    def heads(seg, act):
        return _proj(h, ws[seg], HG_WIDTH, functools.partial(_heads_epilogue, act),
                     [], [], head_shape(HG_HEADS), head_spec(HG_HEADS), tm, "proj_" + seg + "_" + tag)

    qh = heads("qh", "silu")
    vh = heads("vh", "none")
    og = heads("og", "silu")

    def forget(seg, lbp):
        return _proj(h, ws[seg], HG_WIDTH, _forget_epilogue,
                     [lbp], [pl.BlockSpec(lbp.shape, lambda i, j: (0, 0))],
                     [head_shape(HG_HEADS, F32), head_shape(HG_HEADS)],
                     [head_spec(HG_HEADS), head_spec(HG_HEADS)], tm, "proj_" + seg + "_" + tag)

    lf_f, k_f = forget("zf", lb_f)
    lf_b, k_b = forget("zb", lb_b)

    tn = 1024
    gates = _proj(h, ws["gates"], tn, _sigmoid_epilogue, [], [],
                  jax.ShapeDtypeStruct((m, 2 * D_MODEL), BF),
                  pl.BlockSpec((tm, tn), lambda i, j: (i, j)), tm, "proj_gates_" + tag)
    return q, kv, qh, vh, og, lf_f, k_f, lf_b, k_b, gates


def _attn_kernel(q_ref, k_ref, v_ref, o_ref, vt_scr):
    @pl.when(pl.program_id(2) == 0)
    def _():
        vt_scr[0:HEAD_DIM, :] = v_ref[0, 0].T
        vt_scr[HEAD_DIM:, :] = jnp.ones((ATTN_ONES_ROWS, vt_scr.shape[1]), BF)

    tq = q_ref.shape[2]
    keys = k_ref.shape[2]
    n_chunks = keys // ATTN_KEY_CHUNK
    n_pass = GQA_GROUP // ATTN_HEADS_PER_PASS
    qs = [jnp.concatenate([q_ref[0, g] for g in range(ps * ATTN_HEADS_PER_PASS,
                                                       (ps + 1) * ATTN_HEADS_PER_PASS)], axis=0)
          for ps in range(n_pass)]
    items = [(c, ps) for c in range(n_chunks) for ps in range(n_pass)]

    def scores(item):
        c, ps = item
        return _dot_nt(k_ref[0, 0, c * ATTN_KEY_CHUNK:(c + 1) * ATTN_KEY_CHUNK, :], qs[ps])

    state = [None] * n_pass
    s_next = scores(items[0])
    for i, (c, ps) in enumerate(items):
        ks = slice(c * ATTN_KEY_CHUNK, (c + 1) * ATTN_KEY_CHUNK)
        s = s_next
        if i + 1 < len(items):
            s_next = scores(items[i + 1])
        smax = jnp.max(s, axis=0, keepdims=True)
        if c == 0:
            state[ps] = (smax, _dot(vt_scr[:, ks], jnp.exp2(s - smax).astype(BF)))
        else:
            m, acc = state[ps]
            m_new = jnp.maximum(m, smax)
            p = jnp.exp2(s - m_new).astype(BF)
            state[ps] = (m_new, acc * jnp.exp2(m - m_new) + _dot(vt_scr[:, ks], p))
    for ps in range(n_pass):
        acc = state[ps][1]
        o = acc[0:HEAD_DIM] / acc[HEAD_DIM:HEAD_DIM + 1]
        for i in range(ATTN_HEADS_PER_PASS):
            g = ps * ATTN_HEADS_PER_PASS + i
            o_ref[0, :, g * HEAD_DIM:(g + 1) * HEAD_DIM] = o[:, i * tq:(i + 1) * tq].T.astype(BF)


def _attention(q, k, v, tq, name):
    batch, _, seq, _ = q.shape
    keys = k.shape[2]
    kv_spec = pl.BlockSpec((1, 1, keys, HEAD_DIM), lambda b, kv, t: (b, kv, 0, 0))
    return pl.pallas_call(
        _attn_kernel,
        grid=(batch, ATT_KV_HEADS, seq // tq),
        in_specs=[pl.BlockSpec((1, GQA_GROUP, tq, HEAD_DIM), lambda b, kv, t: (b, kv, t, 0)),
                  kv_spec, kv_spec],
        out_specs=pl.BlockSpec((1, tq, GQA_GROUP * HEAD_DIM), lambda b, kv, t: (b, t, kv)),
        out_shape=jax.ShapeDtypeStruct((batch, seq, ATT_WIDTH), BF),
        scratch_shapes=[pltpu.VMEM((HEAD_DIM + ATTN_ONES_ROWS, keys), BF)],
        compiler_params=_params("arbitrary", "arbitrary", "arbitrary"),
        name=name,
    )(q, k, v)


def _split3(x):
    hi = x.astype(BF)
    r1 = x - hi.astype(F32)
    mid = r1.astype(BF)
    lo = (r1 - mid.astype(F32)).astype(BF)
    return hi, mid, lo


def _block_reference(b3, hb, rev):
    n8 = b3.shape[0]
    if hb < 8:
        def row(r):
            return jnp.broadcast_to(b3[:, r:r + 1, :], b3.shape)
        if hb == 4:
            return row(4 if rev else 3)
        assert hb == 2
        sub = lax.broadcasted_iota(jnp.int32, b3.shape, 1)
        r0, r1 = (2, 6) if rev else (1, 5)
        return jnp.where(sub < 4, row(r0), row(r1))
    g = hb // 8
    nblk = n8 // (2 * g)
    edge = (b3[:, 0:1, :] if rev else b3[:, 7:8, :]).reshape(nblk, 2 * g, 1, b3.shape[2])
    pick = edge[:, g:g + 1] if rev else edge[:, g - 1:g]
    return jnp.broadcast_to(pick, (nblk, 2 * g, 8, b3.shape[2])).reshape(b3.shape)


def _chunk_cumsum(lf, tri_b):
    hi, mid, lo = _split3(lf)
    return _dot(tri_b, hi) + _dot(tri_b, mid) + _dot(tri_b, lo)


def _hg_chunk(q, k, lf, b, vt, st, level, rev):
    c = q.shape[0]
    n8 = c // 8
    b3 = b.reshape(n8, 8, HEAD_DIM)
    q32 = q.astype(F32)
    k32 = k.astype(F32)
    sub = lax.broadcasted_iota(jnp.int32, (c, HEAD_DIM), 0)
    qbit = 0 if rev else 1

    def rows8(x, r):
        return x[r * 8:(r + 1) * 8]

    diag = _dot_nt(q, k)
    a_rows = [jnp.where(rows8(level, r) == HG_DIAG_LEVEL, rows8(diag, r), 0.0) for r in range(n8)]
    hb, j = 1, 0
    while hb < c:
        is_query = ((sub & hb) == 0) if rev else ((sub & hb) != 0)
        if hb == 1:
            arg = jnp.where(is_query, lf, 0.0)
        else:
            arg = -jnp.abs(b - _block_reference(b3, hb, rev).reshape(c, HEAD_DIM))
        e = jnp.exp2(arg)
        if hb < 8:
            x = (jnp.where(is_query, q32, k32) * e).astype(BF)
            prod = _dot_nt(x, x)
            q_groups = list(range(n8))
        else:
            q_groups = [r for r in range(n8) if ((r * 8 // hb) & 1) == qbit]
            mixed = jnp.concatenate([rows8(q32 if r in q_groups else k32, r) for r in range(n8)], axis=0)
            y = mixed * e
            lhs = jnp.concatenate([rows8(y, r) for r in q_groups], axis=0).astype(BF)
            prod = _dot_nt(lhs, y.astype(BF))
        for i, r in enumerate(q_groups):
            a_rows[r] = jnp.where(rows8(level, r) == j, rows8(prod, i), a_rows[r])
        hb, j = 2 * hb, j + 1
    a = jnp.concatenate(a_rows, axis=0)
    btot = b[0:1, :] if rev else b[c - 1:c, :]
    qs = (q32 * jnp.exp2(b)).astype(BF)
    ks = (k32 * jnp.exp2(btot - b)).astype(BF)
    o = _dot_nt(jnp.concatenate([a.astype(BF), qs], axis=1),
                jnp.concatenate([vt, st.astype(BF)], axis=1))
    st_new = st * jnp.exp2(btot) + _dot(vt, ks)
    return o, st_new


def _pair_levels(c, rev):
    t = lax.broadcasted_iota(jnp.int32, (c, c), 0)
    s = lax.broadcasted_iota(jnp.int32, (c, c), 1)
    x = t ^ s
    lvl = jnp.zeros((c, c), jnp.int32)
    p = 2
    while p < c:
        lvl = lvl + jnp.where(x >= p, 1, 0)
        p *= 2
    valid = (s > t) if rev else (s < t)
    return jnp.where(valid, lvl, jnp.where(s == t, HG_DIAG_LEVEL, -1))


def _hgrn_kernel(seq, heads, zero_init, q_ref, kf_ref, kb_ref, lf_ref, lb_ref, v_ref, og_ref, *refs):
    if zero_init:
        hn_ref, o_ref, sf_ref, sb_ref, vt_scr, bf_scr, bb_scr, of_scr, ob_scr = refs
    else:
        s0f_ref, s0b_ref, hn_ref, o_ref, vt_scr, bf_scr, bb_scr, of_scr, ob_scr = refs
    c = HG_CHUNK
    n = seq // c
    t = lax.broadcasted_iota(jnp.int32, (c, c), 0)
    s = lax.broadcasted_iota(jnp.int32, (c, c), 1)
    tril_b = jnp.where(s <= t, 1.0, 0.0).astype(BF)
    triu_b = jnp.where(s >= t, 1.0, 0.0).astype(BF)
    level_f = _pair_levels(c, False)
    level_b = _pair_levels(c, True)

    def rows(i):
        return pl.ds(pl.multiple_of(i * c, c), c)

    def head_body(h, _):
        def prepare(i, _):
            r = rows(i)
            vt_scr[i] = v_ref[0, h, r, :].T
            bf_scr[r, :] = _chunk_cumsum(lf_ref[0, h, r, :], tril_b)
            bb_scr[r, :] = _chunk_cumsum(lb_ref[0, h, r, :], triu_b)
            return 0

        lax.fori_loop(0, n, prepare, 0, unroll=2)

        def body(i, carry):
            st_f, st_b = carry
            rf = rows(i)
            rb = rows(n - 1 - i)
            o_f, st_f = _hg_chunk(q_ref[0, h, rf, :], kf_ref[0, h, rf, :], lf_ref[0, h, rf, :],
                                  bf_scr[rf, :], vt_scr[i], st_f, level_f, False)
            of_scr[rf, :] = o_f
            o_b, st_b = _hg_chunk(q_ref[0, h, rb, :], kb_ref[0, h, rb, :], lb_ref[0, h, rb, :],
                                  bb_scr[rb, :], vt_scr[n - 1 - i], st_b, level_b, True)
            ob_scr[rb, :] = o_b
            return st_f, st_b

        if zero_init:
            st0 = (jnp.zeros((HEAD_DIM, HEAD_DIM), F32), jnp.zeros((HEAD_DIM, HEAD_DIM), F32))
        else:
            st0 = (s0f_ref[0, h].T, s0b_ref[0, h].T)
        st_f, st_b = lax.fori_loop(0, n, body, st0, unroll=min(n, HG_UNROLL))
        o = of_scr[...] + ob_scr[...]
        y = _rms(o) * hn_ref[...] * og_ref[0, h].astype(F32)
        o_ref[0, h] = y.astype(BF)
        if zero_init:
            sf_ref[0, h] = st_f.T
            sb_ref[0, h] = st_b.T
        return 0

    lax.fori_loop(0, heads, head_body, 0)


def _hgrn(qh, k_f, k_b, lf_f, lf_b, vh, og, hg_norm, s0f, s0b, heads_per_step, name):
    batch, heads, seq, hd = qh.shape
    hps = heads_per_step
    zero_init = s0f is None
    head_spec = pl.BlockSpec((1, hps, seq, hd), lambda b, h: (b, h, 0, 0))
    state_spec = pl.BlockSpec((1, hps, hd, hd), lambda b, h: (b, h, 0, 0))
    vec_spec = pl.BlockSpec((1, hd), lambda b, h: (0, 0))
    o_shape = jax.ShapeDtypeStruct((batch, heads, seq, hd), BF)
    ins = [qh, k_f, k_b, lf_f, lf_b, vh, og]
    in_specs = [head_spec] * 7
    if zero_init:
        state_shape = jax.ShapeDtypeStruct((batch, heads, hd, hd), F32)
        out_shape = [o_shape, state_shape, state_shape]
        out_specs = [head_spec, state_spec, state_spec]
    else:
        ins += [s0f, s0b]
        in_specs += [state_spec, state_spec]
        out_shape = o_shape
        out_specs = head_spec
    ins.append(hg_norm)
    in_specs.append(vec_spec)
    return pl.pallas_call(
        functools.partial(_hgrn_kernel, seq, hps, zero_init),
        grid=(batch, heads // hps),
        in_specs=in_specs,
        out_specs=out_specs,
        out_shape=out_shape,
        scratch_shapes=[pltpu.VMEM((seq // HG_CHUNK, hd, HG_CHUNK), BF)]
        + [pltpu.VMEM((seq, hd), F32)] * 4,
        compiler_params=_params("arbitrary", "arbitrary"),
        name=name,
    )(*ins)


def _merge_kernel(att_ref, hg_ref, ga_ref, gb_ref, x_ref, wa_ref, wh_ref, wo_ref,
                  g1_ref, sc2_ref, sh2_ref, npost_ref, npre_ref, x1_ref, h2_ref):
    a = _dot(att_ref[...], wa_ref[...])
    hg = jnp.concatenate([hg_ref[0, h] for h in range(HG_HEADS)], axis=1)
    b = _dot(hg, wh_ref[...])
    m = (ga_ref[...].astype(F32) * a + gb_ref[...].astype(F32) * b).astype(BF)
    mo = _dot(m, wo_ref[...])
    x1 = x_ref[...] + g1_ref[0] * (_rms(mo) * npost_ref[...])
    x1_ref[...] = x1
    h2 = _rms(x1) * npre_ref[...]
    h2_ref[...] = (h2 * (1.0 + sc2_ref[0]) + sh2_ref[0]).astype(BF)


def _merge(att, hg, gates, x, wa, wh, wo, g1, sc2, sh2, npost, npre, seq, name):
    m, d = x.shape
    tm = 256
    tpb = seq // tm
    w_att = att.shape[1]
    hg_tpb = hg.shape[2] // tm
    hg_spec = pl.BlockSpec((1, HG_HEADS, tm, HEAD_DIM), lambda i: (i // hg_tpb, 0, i % hg_tpb, 0))

    def const(shape):
        return pl.BlockSpec(shape, lambda i: (0, 0), pipeline_mode=pl.Buffered(1))

    mod_spec = pl.BlockSpec((1, 1, d), lambda i: (i // tpb, 0, 0))
    vec_spec = pl.BlockSpec((1, d), lambda i: (0, 0))
    return pl.pallas_call(
        _merge_kernel,
        grid=(m // tm,),
        in_specs=[
            pl.BlockSpec((tm, w_att), lambda i: (i, 0)),
            hg_spec,
            pl.BlockSpec((tm, d), lambda i: (i, 0)),
            pl.BlockSpec((tm, d), lambda i: (i, 1)),
            pl.BlockSpec((tm, d), lambda i: (i, 0)),
            const(wa.shape), const(wh.shape), const(wo.shape),
            mod_spec, mod_spec, mod_spec, vec_spec, vec_spec,
        ],
        out_specs=[pl.BlockSpec((tm, d), lambda i: (i, 0)), pl.BlockSpec((tm, d), lambda i: (i, 0))],
        out_shape=[jax.ShapeDtypeStruct((m, d), F32), jax.ShapeDtypeStruct((m, d), BF)],
        compiler_params=_params("arbitrary"),
        name=name,
    )(att, hg, gates, gates, x, wa, wh, wo, g1, sc2, sh2, npost, npre)


def _ffn_up_kernel(h_ref, wg_ref, wu_ref, a_ref):
    h = h_ref[...]
    g = _dot(h, wg_ref[...])
    u = _dot(h, wu_ref[...])
    a_ref[...] = (g * jax.nn.sigmoid(g) * u).astype(BF)


def _ffn_down_kernel(a_ref, wo_ref, x1_ref, g2_ref, npost_ref, o_ref):
    y = _rms(_dot(a_ref[...], wo_ref[...])) * npost_ref[...]
    o_ref[...] = x1_ref[...] + g2_ref[0] * y


def _ffn(h2, x1, w_in, w_out, g2, npost, seq, name):
    m, d = x1.shape
    tm, tf = PROJ_TOKENS, 512
    nf = D_FF // tf
    act = pl.pallas_call(
        _ffn_up_kernel,
        grid=(m // tm, nf),
        in_specs=[
            pl.BlockSpec((tm, d), lambda i, f: (i, 0)),
            pl.BlockSpec((d, tf), lambda i, f: (0, f)),
            pl.BlockSpec((d, tf), lambda i, f: (0, nf + f)),
        ],
        out_specs=pl.BlockSpec((tm, tf), lambda i, f: (i, f)),
        out_shape=jax.ShapeDtypeStruct((m, D_FF), BF),
        compiler_params=_params("arbitrary", "arbitrary"),
        name=name + "_up",
    )(h2, w_in, w_in)
    td = 256
    tpb = seq // td
    return pl.pallas_call(
        _ffn_down_kernel,
        grid=(m // td,),
        in_specs=[
            pl.BlockSpec((td, D_FF), lambda i: (i, 0)),
            pl.BlockSpec((D_FF, d), lambda i: (0, 0), pipeline_mode=pl.Buffered(1)),
            pl.BlockSpec((td, d), lambda i: (i, 0)),
            pl.BlockSpec((1, 1, d), lambda i: (i // tpb, 0, 0)),
            pl.BlockSpec((1, d), lambda i: (0, 0)),
        ],
        out_specs=pl.BlockSpec((td, d), lambda i: (i, 0)),
        out_shape=jax.ShapeDtypeStruct((m, d), F32),
        compiler_params=_params("arbitrary"),
        name=name + "_down",
    )(act, w_out, x1, g2, npost)


def _rope_tables(n_tokens):
    rows = n_tokens // GRID_W
    half = HEAD_DIM // 4
    r = jnp.repeat(jnp.arange(rows), GRID_W).astype(F32)
    col = jnp.tile(jnp.arange(GRID_W), rows).astype(F32)
    inv = ROPE_THETA ** (-jnp.arange(half, dtype=F32) / half)
    ar = r[:, None] * inv
    ac = col[:, None] * inv
    cos = jnp.concatenate([jnp.cos(ar), jnp.cos(ar), jnp.cos(ac), jnp.cos(ac)], axis=-1)
    sin = jnp.concatenate([-jnp.sin(ar), jnp.sin(ar), -jnp.sin(ac), jnp.sin(ac)], axis=-1)
    return cos, sin


def _layer(x3, mods, weights, rope_tabs, cache, states, tag):
    batch, seq, d = x3.shape
    m = batch * seq
    x = x3.reshape(m, d)
    (w_in, q_norm, k_norm, lb_f, lb_b, hg_norm, wa, wh, wo, w_ffn_in, w_ffn_out,
     n_pre_mix, n_post_mix, n_pre_ffn, n_post_ffn) = weights
    nb = mods.shape[0]
    sh1, sc1, g1, sh2, sc2, g2 = [mods[:, i].reshape(nb, 1, d) for i in range(6)]
    seq_mod = seq if nb == batch else m

    h = _modnorm(x, n_pre_mix, sc1, sh1, seq_mod)
    q, kv, qh, vh, og, lf_f, k_f, lf_b, k_b, gates = _in_projection(
        h, w_in, batch, seq, rope_tabs, q_norm, k_norm, lb_f, lb_b, tag)

    if cache is None:
        k_bf, v_bf, k_f32, v_f32 = kv
        att = _attention(q, k_bf, v_bf, seq, "attn_" + tag)
    else:
        k_bf, v_bf = kv
        att = _attention(q, jnp.concatenate([k_bf, cache[0]], axis=2),
                         jnp.concatenate([v_bf, cache[1]], axis=2), 256, "attn_" + tag)
        k_f32 = v_f32 = None

    hps = HG_HEADS if seq <= 2 * HG_CHUNK else 1
    if states is None:
        hg, s_f, s_b = _hgrn(qh, k_f, k_b, lf_f, lf_b, vh, og, hg_norm, None, None, hps, "hgrn_" + tag)
    else:
        hg = _hgrn(qh, k_f, k_b, lf_f, lf_b, vh, og, hg_norm, states[0], states[1], hps, "hgrn_" + tag)
        s_f = s_b = None

    x1, h2 = _merge(att.reshape(m, ATT_WIDTH), hg, gates, x, wa, wh, wo,
                    g1, sc2, sh2, n_post_mix, n_pre_ffn, seq_mod, "merge_" + tag)
    y = _ffn(h2, x1, w_ffn_in, w_ffn_out, g2, n_post_ffn, seq_mod, "ffn_" + tag)
    return y.reshape(batch, seq, d), k_f32, v_f32, s_f, s_b


def kernel(x_prompt, x_sample, cache_k, cache_v, state_fwd, state_bwd, c, c_ctx, w_ada, b_ada,
           norm_pre_mix, norm_post_mix, norm_pre_ffn, norm_post_ffn, w_in, q_norm, k_norm,
           lb_fwd, lb_bwd, hg_norm, w_br_att, w_br_hg, w_out, w_ffn_in, w_ffn_out):
    depth = w_in.shape[0]
    assert depth == 1 and lb_fwd.shape[0] == 2
    batch, seq, d = x_prompt.shape
    dec_batch, dec_seq, _ = x_sample.shape
    past = cache_k.shape[2]

    cond = jnp.concatenate([c_ctx[None, :], c, jnp.zeros((8 - 1 - dec_batch, d), F32)], axis=0)
    mods = _ada(cond, w_ada[0], b_ada[0][None, :]).reshape(8, 6, d)

    segments = (("q", OFF_Q, OFF_KV), ("kv", OFF_KV, OFF_QH), ("qh", OFF_QH, OFF_ZF),
                ("zf", OFF_ZF, OFF_ZB), ("zb", OFF_ZB, OFF_VH), ("vh", OFF_VH, OFF_OG),
                ("og", OFF_OG, OFF_GA), ("gates", OFF_GA, OFF_GB + D_MODEL))
    w_in_segments = {name: w_in[0, :, a:b].astype(BF) for name, a, b in segments}
    weights = (
        w_in_segments, q_norm, k_norm, lb_fwd, lb_bwd, hg_norm,
        w_br_att[0].astype(BF), w_br_hg[0].astype(BF), w_out[0].astype(BF),
        w_ffn_in[0].astype(BF), w_ffn_out[0].astype(BF),
        norm_pre_mix, norm_post_mix, norm_pre_ffn, norm_post_ffn,
    )

    y_p, k_c, v_c, s_f, s_b = _layer(x_prompt, mods[0:1], weights, None, None, None, "ctx")

    cache = (
        cache_k[:, 0].transpose(0, 2, 1, 3).astype(BF),
        cache_v[:, 0].transpose(0, 2, 1, 3).astype(BF),
    )
    states = (state_fwd[:, 0], state_bwd[:, 0])
    y_s, _, _, _, _ = _layer(x_sample, mods[1:1 + dec_batch], weights, _rope_tables(dec_seq),
                             cache, states, "lat")

    new_k = k_c.reshape(batch, 1, seq, ATT_KV_HEADS, HEAD_DIM)
    new_v = v_c.reshape(batch, 1, seq, ATT_KV_HEADS, HEAD_DIM)
    return (y_p, y_s, new_k, new_v, s_f[:, None], s_b[:, None])
```

```python
import functools

import jax
import jax.numpy as jnp
from jax import lax
from jax.experimental import pallas as pl
from jax.experimental.pallas import tpu as pltpu

D_MODEL = 2048
HEAD_DIM = 128
ATT_HEADS = 8
ATT_KV_HEADS = 2
GQA_GROUP = ATT_HEADS // ATT_KV_HEADS
ATT_WIDTH = ATT_HEADS * HEAD_DIM
KV_WIDTH = ATT_KV_HEADS * HEAD_DIM
HG_HEADS = 8
HG_WIDTH = HG_HEADS * HEAD_DIM
D_FF = 5632
GRID_W = 64
ROPE_THETA = 10000.0
NORM_EPS = 1e-6
HG_CHUNK = 128
HG_DIAG_LEVEL = 31
HG_UNROLL = 4
ATTN_KEY_CHUNK = 256
ATTN_HEADS_PER_PASS = 4
ATTN_ONES_ROWS = 16
LOG2E = 1.4426950408889634

OFF_Q = 0
OFF_KV = ATT_WIDTH
OFF_QH = OFF_KV + 2 * KV_WIDTH
OFF_ZF = OFF_QH + HG_WIDTH
OFF_ZB = OFF_ZF + HG_WIDTH
OFF_VH = OFF_ZB + HG_WIDTH
OFF_OG = OFF_VH + HG_WIDTH
OFF_GA = OFF_OG + HG_WIDTH
OFF_GB = OFF_GA + D_MODEL

BF = jnp.bfloat16
F32 = jnp.float32

VMEM_LIMIT_BYTES = 56 * 1024 * 1024


def _params(*sem):
    return pltpu.CompilerParams(dimension_semantics=sem, vmem_limit_bytes=VMEM_LIMIT_BYTES)


def _rms(x):
    return x * lax.rsqrt(jnp.mean(x * x, axis=-1, keepdims=True) + NORM_EPS)


def _dot(a, b):
    return jnp.dot(a, b, preferred_element_type=F32)


def _dot_nt(a, b):
    return lax.dot_general(a, b, (((1,), (1,)), ((), ())), preferred_element_type=F32)


def _dot_tn(a, b):
    return lax.dot_general(a, b, (((0,), (0,)), ((), ())), preferred_element_type=F32)


def _ada_kernel(c_ref, w_ref, b_ref, o_ref):
    c = c_ref[...]
    s = (c * jax.nn.sigmoid(c)).astype(BF)
    o_ref[...] = _dot(s, w_ref[...].astype(BF)) + b_ref[...]


def _ada(cond, w, b):
    rows, d = cond.shape
    n = w.shape[1]
    tn = 1536
    return pl.pallas_call(
        _ada_kernel,
        grid=(n // tn,),
        in_specs=[
            pl.BlockSpec((rows, d), lambda j: (0, 0)),
            pl.BlockSpec((d, tn), lambda j: (0, j)),
            pl.BlockSpec((1, tn), lambda j: (0, j)),
        ],
        out_specs=pl.BlockSpec((rows, tn), lambda j: (0, j)),
        out_shape=jax.ShapeDtypeStruct((rows, n), F32),
        compiler_params=_params("arbitrary"),
        name="ada_mod",
    )(cond, w, b)


def _modnorm_kernel(x_ref, g_ref, sc_ref, sh_ref, o_ref):
    y = _rms(x_ref[...]) * g_ref[...]
    o_ref[...] = (y * (1.0 + sc_ref[0]) + sh_ref[0]).astype(BF)


def _modnorm(x, g, sc, sh, seq):
    m, d = x.shape
    tm = 512
    tpb = seq // tm
    return pl.pallas_call(
        _modnorm_kernel,
        grid=(m // tm,),
        in_specs=[
            pl.BlockSpec((tm, d), lambda i: (i, 0)),
            pl.BlockSpec((1, d), lambda i: (0, 0)),
            pl.BlockSpec((1, 1, d), lambda i: (i // tpb, 0, 0)),
            pl.BlockSpec((1, 1, d), lambda i: (i // tpb, 0, 0)),
        ],
        out_specs=pl.BlockSpec((tm, d), lambda i: (i, 0)),
        out_shape=jax.ShapeDtypeStruct((m, d), BF),
        compiler_params=_params("arbitrary"),
        name="modnorm",
    )(x, g, sc, sh)


def _head_rms(x):
    ones = jnp.ones((HEAD_DIM, HEAD_DIM), BF)
    ss = _dot((x * x).astype(BF), ones)
    return x * lax.rsqrt(ss * (1.0 / HEAD_DIM) + NORM_EPS)


def _rope(y, cos_ref, sin_ref):
    i = lax.broadcasted_iota(jnp.int32, (HEAD_DIM, HEAD_DIM), 0)
    j = lax.broadcasted_iota(jnp.int32, (HEAD_DIM, HEAD_DIM), 1)
    partner = jnp.where(i == (j ^ 32), 1.0, 0.0).astype(BF)
    return y * cos_ref[...] + _dot(y.astype(BF), partner) * sin_ref[...]


def _store_heads(o_ref, h, y):
    nbb, _, ts, _ = o_ref.shape
    for bl in range(nbb):
        o_ref[bl, h] = y[bl * ts:(bl + 1) * ts].astype(o_ref.dtype)


def _q_epilogue(rope, acc, col0, refs):
    if rope:
        qn_ref, cos_ref, sin_ref, o_ref = refs
    else:
        qn_ref, o_ref = refs
    scale = LOG2E * HEAD_DIM ** -0.5
    for hl in range(acc.shape[1] // HEAD_DIM):
        y = _head_rms(acc[:, hl * HEAD_DIM:(hl + 1) * HEAD_DIM]) * qn_ref[...]
        if rope:
            y = _rope(y, cos_ref, sin_ref)
        _store_heads(o_ref, col0 // HEAD_DIM + hl, y * scale)


def _kv_epilogue(rope, acc, col0, refs):
    if rope:
        kn_ref, cos_ref, sin_ref, kb_ref, vb_ref = refs
    else:
        kn_ref, kb_ref, vb_ref, kf_ref, vf_ref = refs
    is_k = col0 < KV_WIDTH
    for hl in range(acc.shape[1] // HEAD_DIM):
        h = (col0 % KV_WIDTH) // HEAD_DIM + hl
        sl = slice(h * HEAD_DIM, (h + 1) * HEAD_DIM)
        y = acc[:, hl * HEAD_DIM:(hl + 1) * HEAD_DIM]
        if is_k:
            y = _head_rms(y) * kn_ref[...]
        if not rope:
            (kf_ref if is_k else vf_ref)[:, sl] = y
        elif is_k:
            y = _rope(y, cos_ref, sin_ref)
        _store_heads(kb_ref if is_k else vb_ref, h, y)


def _heads_epilogue(act, acc, col0, refs):
    (o_ref,) = refs
    for hl in range(acc.shape[1] // HEAD_DIM):
        y = acc[:, hl * HEAD_DIM:(hl + 1) * HEAD_DIM]
        if act == "silu":
            y = y * jax.nn.sigmoid(y)
        _store_heads(o_ref, col0 // HEAD_DIM + hl, y)


def _forget_epilogue(acc, col0, refs):
    lbp_ref, lf_ref, k_ref = refs
    p = lbp_ref[:, col0:col0 + acc.shape[1]]
    e = jnp.exp(p - jnp.max(p, axis=0, keepdims=True))
    lb = e[0:1] / jnp.sum(e, axis=0, keepdims=True)
    z = acc
    t = jnp.exp(-jnp.abs(z))
    r = 1.0 / (1.0 + t)
    tr = t * r
    pos = z >= 0
    sig_p = jnp.where(pos, r, tr)
    sig_n = jnp.where(pos, tr, r)
    logf = jnp.log2(lb + (1.0 - lb) * sig_p)
    kk = (1.0 - lb) * sig_n
    for hl in range(acc.shape[1] // HEAD_DIM):
        sl = slice(hl * HEAD_DIM, (hl + 1) * HEAD_DIM)
        _store_heads(lf_ref, col0 // HEAD_DIM + hl, logf[:, sl])
        _store_heads(k_ref, col0 // HEAD_DIM + hl, kk[:, sl])


def _sigmoid_epilogue(acc, col0, refs):
    (o_ref,) = refs
    o_ref[:, col0:col0 + acc.shape[1]] = jax.nn.sigmoid(acc).astype(BF)


def _proj_kernel(epilogue, h_ref, w_ref, *refs):
    h = h_ref[...]
    starts = list(range(0, w_ref.shape[1], PROJ_COL_GROUP))

    def product(c0):
        return _dot(h, w_ref[:, c0:c0 + PROJ_COL_GROUP])

    nxt = product(starts[0])
    for i, c0 in enumerate(starts):
        acc = nxt
        if i + 1 < len(starts):
            nxt = product(starts[i + 1])
        epilogue(acc, c0, refs)


def _proj(h, w, tn, epilogue, extra, extra_specs, out_shapes, out_specs, tm, name):
    m, d = h.shape
    ncols = w.shape[1]
    return pl.pallas_call(
        functools.partial(_proj_kernel, epilogue),
        grid=(m // tm, ncols // tn),
        in_specs=[
            pl.BlockSpec((tm, d), lambda i, j: (i, 0)),
            pl.BlockSpec((d, tn), lambda i, j: (0, j)),
        ] + extra_specs,
        out_specs=out_specs,
        out_shape=out_shapes,
        compiler_params=_params("arbitrary", "arbitrary"),
        name=name,
    )(h, w, *extra)


PROJ_TOKENS = 1024
PROJ_COL_GROUP = 256


def _in_projection(h, ws, batch, seq, rope_tabs, q_norm, k_norm, lb_f, lb_b, tag):
    m = h.shape[0]
    tm = PROJ_TOKENS
    rope = rope_tabs is not None
    if seq >= tm:
        tpb = seq // tm
        head_block = (1, tm)

        def head_index(i, j):
            return (i // tpb, 0, i % tpb, 0)
    else:
        tpb = 1
        head_block = (tm // seq, seq)

        def head_index(i, j):
            return (i, 0, 0, 0)

    def head_spec(nh):
        return pl.BlockSpec((head_block[0], nh, head_block[1], HEAD_DIM), head_index)

    def head_shape(nh, dt=BF):
        return jax.ShapeDtypeStruct((batch, nh, seq, HEAD_DIM), dt)

    vec_spec = pl.BlockSpec((1, HEAD_DIM), lambda i, j: (0, 0))
    tab_spec = pl.BlockSpec((tm, HEAD_DIM), lambda i, j: (i % tpb, 0))
    rope_in = list(rope_tabs) if rope else []
    rope_specs = [tab_spec, tab_spec] if rope else []

    q = _proj(h, ws["q"], ATT_WIDTH, functools.partial(_q_epilogue, rope),
              [q_norm] + rope_in, [vec_spec] + rope_specs,
              head_shape(ATT_HEADS), head_spec(ATT_HEADS), tm, "proj_q_" + tag)

    kv_shapes = [head_shape(ATT_KV_HEADS), head_shape(ATT_KV_HEADS)]
    kv_specs = [head_spec(ATT_KV_HEADS), head_spec(ATT_KV_HEADS)]
    if not rope:
        tok_spec = pl.BlockSpec((tm, KV_WIDTH), lambda i, j: (i, 0))
        kv_shapes += [jax.ShapeDtypeStruct((m, KV_WIDTH), F32)] * 2
        kv_specs += [tok_spec, tok_spec]
    kv = _proj(h, ws["kv"], 2 * KV_WIDTH, functools.partial(_kv_epilogue, rope),
               [k_norm] + rope_in, [vec_spec] + rope_specs,
               kv_shapes, kv_specs, tm, "proj_kv_" + tag)

    def heads(seg, act):
        return _proj(h, ws[seg], HG_WIDTH, functools.partial(_heads_epilogue, act),
                     [], [], head_shape(HG_HEADS), head_spec(HG_HEADS), tm, "proj_" + seg + "_" + tag)

    qh = heads("qh", "silu")
    vh = heads("vh", "none")
    og = heads("og", "silu")

    def forget(seg, lbp):
        return _proj(h, ws[seg], HG_WIDTH, _forget_epilogue,
                     [lbp], [pl.BlockSpec(lbp.shape, lambda i, j: (0, 0))],
                     [head_shape(HG_HEADS, F32), head_shape(HG_HEADS)],
                     [head_spec(HG_HEADS), head_spec(HG_HEADS)], tm, "proj_" + seg + "_" + tag)

    lf_f, k_f = forget("zf", lb_f)
    lf_b, k_b = forget("zb", lb_b)

    tn = 1024
    gates = _proj(h, ws["gates"], tn, _sigmoid_epilogue, [], [],
                  jax.ShapeDtypeStruct((m, 2 * D_MODEL), BF),
                  pl.BlockSpec((tm, tn), lambda i, j: (i, j)), tm, "proj_gates_" + tag)
    return q, kv, qh, vh, og, lf_f, k_f, lf_b, k_b, gates


def _attn_kernel(q_ref, k_ref, v_ref, o_ref, vt_scr):
    @pl.when(pl.program_id(2) == 0)
    def _():
        vt_scr[0:HEAD_DIM, :] = v_ref[0, 0].T
        vt_scr[HEAD_DIM:, :] = jnp.ones((ATTN_ONES_ROWS, vt_scr.shape[1]), BF)

    tq = q_ref.shape[2]
    keys = k_ref.shape[2]
    n_chunks = keys // ATTN_KEY_CHUNK
    n_pass = GQA_GROUP // ATTN_HEADS_PER_PASS
    qs = [jnp.concatenate([q_ref[0, g] for g in range(ps * ATTN_HEADS_PER_PASS,
                                                       (ps + 1) * ATTN_HEADS_PER_PASS)], axis=0)
          for ps in range(n_pass)]
    items = [(c, ps) for c in range(n_chunks) for ps in range(n_pass)]

    def scores(item):
        c, ps = item
        return _dot_nt(k_ref[0, 0, c * ATTN_KEY_CHUNK:(c + 1) * ATTN_KEY_CHUNK, :], qs[ps])

    state = [None] * n_pass
    s_next = scores(items[0])
    for i, (c, ps) in enumerate(items):
        ks = slice(c * ATTN_KEY_CHUNK, (c + 1) * ATTN_KEY_CHUNK)
        s = s_next
        if i + 1 < len(items):
            s_next = scores(items[i + 1])
        smax = jnp.max(s, axis=0, keepdims=True)
        if c == 0:
            state[ps] = (smax, _dot(vt_scr[:, ks], jnp.exp2(s - smax).astype(BF)))
        else:
            m, acc = state[ps]
            m_new = jnp.maximum(m, smax)
            p = jnp.exp2(s - m_new).astype(BF)
            state[ps] = (m_new, acc * jnp.exp2(m - m_new) + _dot(vt_scr[:, ks], p))
    for ps in range(n_pass):
        acc = state[ps][1]
        o = acc[0:HEAD_DIM] / acc[HEAD_DIM:HEAD_DIM + 1]
        for i in range(ATTN_HEADS_PER_PASS):
            g = ps * ATTN_HEADS_PER_PASS + i
            o_ref[0, :, g * HEAD_DIM:(g + 1) * HEAD_DIM] = o[:, i * tq:(i + 1) * tq].T.astype(BF)


def _attention(q, k, v, tq, name):
    batch, _, seq, _ = q.shape
    keys = k.shape[2]
    kv_spec = pl.BlockSpec((1, 1, keys, HEAD_DIM), lambda b, kv, t: (b, kv, 0, 0))
    return pl.pallas_call(
        _attn_kernel,
        grid=(batch, ATT_KV_HEADS, seq // tq),
        in_specs=[pl.BlockSpec((1, GQA_GROUP, tq, HEAD_DIM), lambda b, kv, t: (b, kv, t, 0)),
                  kv_spec, kv_spec],
        out_specs=pl.BlockSpec((1, tq, GQA_GROUP * HEAD_DIM), lambda b, kv, t: (b, t, kv)),
        out_shape=jax.ShapeDtypeStruct((batch, seq, ATT_WIDTH), BF),
        scratch_shapes=[pltpu.VMEM((HEAD_DIM + ATTN_ONES_ROWS, keys), BF)],
        compiler_params=_params("arbitrary", "arbitrary", "arbitrary"),
        name=name,
    )(q, k, v)


def _split3(x):
    hi = x.astype(BF)
    r1 = x - hi.astype(F32)
    mid = r1.astype(BF)
    lo = (r1 - mid.astype(F32)).astype(BF)
    return hi, mid, lo


def _block_reference(b3, hb, rev):
    n8 = b3.shape[0]
    if hb < 8:
        def row(r):
            return jnp.broadcast_to(b3[:, r:r + 1, :], b3.shape)
        if hb == 4:
            return row(4 if rev else 3)
        assert hb == 2
        sub = lax.broadcasted_iota(jnp.int32, b3.shape, 1)
        r0, r1 = (2, 6) if rev else (1, 5)
        return jnp.where(sub < 4, row(r0), row(r1))
    g = hb // 8
    nblk = n8 // (2 * g)
    edge = (b3[:, 0:1, :] if rev else b3[:, 7:8, :]).reshape(nblk, 2 * g, 1, b3.shape[2])
    pick = edge[:, g:g + 1] if rev else edge[:, g - 1:g]
    return jnp.broadcast_to(pick, (nblk, 2 * g, 8, b3.shape[2])).reshape(b3.shape)


def _chunk_cumsum(lf, tri_b):
    hi, mid, lo = _split3(lf)
    return _dot(tri_b, hi) + _dot(tri_b, mid) + _dot(tri_b, lo)


def _hg_chunk(q, k, lf, b, vt, st, level, rev):
    c = q.shape[0]
    n8 = c // 8
    b3 = b.reshape(n8, 8, HEAD_DIM)
    q32 = q.astype(F32)
    k32 = k.astype(F32)
    sub = lax.broadcasted_iota(jnp.int32, (c, HEAD_DIM), 0)
    qbit = 0 if rev else 1

    def rows8(x, r):
        return x[r * 8:(r + 1) * 8]

    diag = _dot_nt(q, k)
    a_rows = [jnp.where(rows8(level, r) == HG_DIAG_LEVEL, rows8(diag, r), 0.0) for r in range(n8)]
    hb, j = 1, 0
    while hb < c:
        is_query = ((sub & hb) == 0) if rev else ((sub & hb) != 0)
        if hb == 1:
            arg = jnp.where(is_query, lf, 0.0)
        else:
            arg = -jnp.abs(b - _block_reference(b3, hb, rev).reshape(c, HEAD_DIM))
        e = jnp.exp2(arg)
        if hb < 8:
            x = (jnp.where(is_query, q32, k32) * e).astype(BF)
            prod = _dot_nt(x, x)
            q_groups = list(range(n8))
        else:
            q_groups = [r for r in range(n8) if ((r * 8 // hb) & 1) == qbit]
            mixed = jnp.concatenate([rows8(q32 if r in q_groups else k32, r) for r in range(n8)], axis=0)
            y = mixed * e
            lhs = jnp.concatenate([rows8(y, r) for r in q_groups], axis=0).astype(BF)
            prod = _dot_nt(lhs, y.astype(BF))
        for i, r in enumerate(q_groups):
            a_rows[r] = jnp.where(rows8(level, r) == j, rows8(prod, i), a_rows[r])
        hb, j = 2 * hb, j + 1
    a = jnp.concatenate(a_rows, axis=0)
    btot = b[0:1, :] if rev else b[c - 1:c, :]
    qs = (q32 * jnp.exp2(b)).astype(BF)
    ks = (k32 * jnp.exp2(btot - b)).astype(BF)
    o = _dot_nt(jnp.concatenate([a.astype(BF), qs], axis=1),
                jnp.concatenate([vt, st.astype(BF)], axis=1))
    st_new = st * jnp.exp2(btot) + _dot(vt, ks)
    return o, st_new


def _pair_levels(c, rev):
    t = lax.broadcasted_iota(jnp.int32, (c, c), 0)
    s = lax.broadcasted_iota(jnp.int32, (c, c), 1)
    x = t ^ s
    lvl = jnp.zeros((c, c), jnp.int32)
    p = 2
    while p < c:
        lvl = lvl + jnp.where(x >= p, 1, 0)
        p *= 2
    valid = (s > t) if rev else (s < t)
    return jnp.where(valid, lvl, jnp.where(s == t, HG_DIAG_LEVEL, -1))


def _hgrn_kernel(seq, heads, zero_init, q_ref, kf_ref, kb_ref, lf_ref, lb_ref, v_ref, og_ref, *refs):
    if zero_init:
        hn_ref, o_ref, sf_ref, sb_ref, vt_scr, bf_scr, bb_scr, of_scr, ob_scr = refs
    else:
        s0f_ref, s0b_ref, hn_ref, o_ref, vt_scr, bf_scr, bb_scr, of_scr, ob_scr = refs
    c = HG_CHUNK
    n = seq // c
    t = lax.broadcasted_iota(jnp.int32, (c, c), 0)
    s = lax.broadcasted_iota(jnp.int32, (c, c), 1)
    tril_b = jnp.where(s <= t, 1.0, 0.0).astype(BF)
    triu_b = jnp.where(s >= t, 1.0, 0.0).astype(BF)
    level_f = _pair_levels(c, False)
    level_b = _pair_levels(c, True)

    def rows(i):
        return pl.ds(pl.multiple_of(i * c, c), c)

    def head_body(h, _):
        def prepare(i, _):
            r = rows(i)
            vt_scr[i] = v_ref[0, h, r, :].T
            bf_scr[r, :] = _chunk_cumsum(lf_ref[0, h, r, :], tril_b)
            bb_scr[r, :] = _chunk_cumsum(lb_ref[0, h, r, :], triu_b)
            return 0

        lax.fori_loop(0, n, prepare, 0, unroll=2)

        def body(i, carry):
            st_f, st_b = carry
            rf = rows(i)
            rb = rows(n - 1 - i)
            o_f, st_f = _hg_chunk(q_ref[0, h, rf, :], kf_ref[0, h, rf, :], lf_ref[0, h, rf, :],
                                  bf_scr[rf, :], vt_scr[i], st_f, level_f, False)
            of_scr[rf, :] = o_f
            o_b, st_b = _hg_chunk(q_ref[0, h, rb, :], kb_ref[0, h, rb, :], lb_ref[0, h, rb, :],
                                  bb_scr[rb, :], vt_scr[n - 1 - i], st_b, level_b, True)
            ob_scr[rb, :] = o_b
            return st_f, st_b

        if zero_init:
            st0 = (jnp.zeros((HEAD_DIM, HEAD_DIM), F32), jnp.zeros((HEAD_DIM, HEAD_DIM), F32))
        else:
            st0 = (s0f_ref[0, h].T, s0b_ref[0, h].T)
        st_f, st_b = lax.fori_loop(0, n, body, st0, unroll=min(n, HG_UNROLL))
        o = of_scr[...] + ob_scr[...]
        y = _rms(o) * hn_ref[...] * og_ref[0, h].astype(F32)
        o_ref[0, h] = y.astype(BF)
        if zero_init:
            sf_ref[0, h] = st_f.T
            sb_ref[0, h] = st_b.T
        return 0

    lax.fori_loop(0, heads, head_body, 0)


def _hgrn(qh, k_f, k_b, lf_f, lf_b, vh, og, hg_norm, s0f, s0b, heads_per_step, name):
    batch, heads, seq, hd = qh.shape
    hps = heads_per_step
    zero_init = s0f is None
    head_spec = pl.BlockSpec((1, hps, seq, hd), lambda b, h: (b, h, 0, 0))
    state_spec = pl.BlockSpec((1, hps, hd, hd), lambda b, h: (b, h, 0, 0))
    vec_spec = pl.BlockSpec((1, hd), lambda b, h: (0, 0))
    o_shape = jax.ShapeDtypeStruct((batch, heads, seq, hd), BF)
    ins = [qh, k_f, k_b, lf_f, lf_b, vh, og]
    in_specs = [head_spec] * 7
    if zero_init:
        state_shape = jax.ShapeDtypeStruct((batch, heads, hd, hd), F32)
        out_shape = [o_shape, state_shape, state_shape]
        out_specs = [head_spec, state_spec, state_spec]
    else:
        ins += [s0f, s0b]
        in_specs += [state_spec, state_spec]
        out_shape = o_shape
        out_specs = head_spec
    ins.append(hg_norm)
    in_specs.append(vec_spec)
    return pl.pallas_call(
        functools.partial(_hgrn_kernel, seq, hps, zero_init),
        grid=(batch, heads // hps),
        in_specs=in_specs,
        out_specs=out_specs,
        out_shape=out_shape,
        scratch_shapes=[pltpu.VMEM((seq // HG_CHUNK, hd, HG_CHUNK), BF)]
        + [pltpu.VMEM((seq, hd), F32)] * 4,
        compiler_params=_params("arbitrary", "arbitrary"),
        name=name,
    )(*ins)


def _merge_kernel(att_ref, hg_ref, ga_ref, gb_ref, x_ref, wa_ref, wh_ref, wo_ref,
                  g1_ref, sc2_ref, sh2_ref, npost_ref, npre_ref, x1_ref, h2_ref):
    a = _dot(att_ref[...], wa_ref[...])
    hg = jnp.concatenate([hg_ref[0, h] for h in range(HG_HEADS)], axis=1)
    b = _dot(hg, wh_ref[...])
    m = (ga_ref[...].astype(F32) * a + gb_ref[...].astype(F32) * b).astype(BF)
    mo = _dot(m, wo_ref[...])
    x1 = x_ref[...] + g1_ref[0] * (_rms(mo) * npost_ref[...])
    x1_ref[...] = x1
    h2 = _rms(x1) * npre_ref[...]
    h2_ref[...] = (h2 * (1.0 + sc2_ref[0]) + sh2_ref[0]).astype(BF)


def _merge(att, hg, gates, x, wa, wh, wo, g1, sc2, sh2, npost, npre, seq, name):
    m, d = x.shape
    tm = 256
    tpb = seq // tm
    w_att = att.shape[1]
    hg_tpb = hg.shape[2] // tm
    hg_spec = pl.BlockSpec((1, HG_HEADS, tm, HEAD_DIM), lambda i: (i // hg_tpb, 0, i % hg_tpb, 0))

    def const(shape):
        return pl.BlockSpec(shape, lambda i: (0, 0), pipeline_mode=pl.Buffered(1))

    mod_spec = pl.BlockSpec((1, 1, d), lambda i: (i // tpb, 0, 0))
    vec_spec = pl.BlockSpec((1, d), lambda i: (0, 0))
    return pl.pallas_call(
        _merge_kernel,
        grid=(m // tm,),
        in_specs=[
            pl.BlockSpec((tm, w_att), lambda i: (i, 0)),
            hg_spec,
            pl.BlockSpec((tm, d), lambda i: (i, 0)),
            pl.BlockSpec((tm, d), lambda i: (i, 1)),
            pl.BlockSpec((tm, d), lambda i: (i, 0)),
            const(wa.shape), const(wh.shape), const(wo.shape),
            mod_spec, mod_spec, mod_spec, vec_spec, vec_spec,
        ],
        out_specs=[pl.BlockSpec((tm, d), lambda i: (i, 0)), pl.BlockSpec((tm, d), lambda i: (i, 0))],
        out_shape=[jax.ShapeDtypeStruct((m, d), F32), jax.ShapeDtypeStruct((m, d), BF)],
        compiler_params=_params("arbitrary"),
        name=name,
    )(att, hg, gates, gates, x, wa, wh, wo, g1, sc2, sh2, npost, npre)


def _ffn_up_kernel(h_ref, wg_ref, wu_ref, a_ref):
    h = h_ref[...]
    g = _dot(h, wg_ref[...])
    u = _dot(h, wu_ref[...])
    a_ref[...] = (g * jax.nn.sigmoid(g) * u).astype(BF)


def _ffn_down_kernel(a_ref, wo_ref, x1_ref, g2_ref, npost_ref, o_ref):
    y = _rms(_dot(a_ref[...], wo_ref[...])) * npost_ref[...]
    o_ref[...] = x1_ref[...] + g2_ref[0] * y


def _ffn(h2, x1, w_in, w_out, g2, npost, seq, name):
    m, d = x1.shape
    tm, tf = PROJ_TOKENS, 512
    nf = D_FF // tf
    act = pl.pallas_call(
        _ffn_up_kernel,
        grid=(m // tm, nf),
        in_specs=[
            pl.BlockSpec((tm, d), lambda i, f: (i, 0)),
            pl.BlockSpec((d, tf), lambda i, f: (0, f)),
            pl.BlockSpec((d, tf), lambda i, f: (0, nf + f)),
        ],
        out_specs=pl.BlockSpec((tm, tf), lambda i, f: (i, f)),
        out_shape=jax.ShapeDtypeStruct((m, D_FF), BF),
        compiler_params=_params("arbitrary", "arbitrary"),
        name=name + "_up",
    )(h2, w_in, w_in)
    td = 256
    tpb = seq // td
    return pl.pallas_call(
        _ffn_down_kernel,
        grid=(m // td,),
        in_specs=[
            pl.BlockSpec((td, D_FF), lambda i: (i, 0)),
            pl.BlockSpec((D_FF, d), lambda i: (0, 0), pipeline_mode=pl.Buffered(1)),
            pl.BlockSpec((td, d), lambda i: (i, 0)),
            pl.BlockSpec((1, 1, d), lambda i: (i // tpb, 0, 0)),
            pl.BlockSpec((1, d), lambda i: (0, 0)),
        ],
        out_specs=pl.BlockSpec((td, d), lambda i: (i, 0)),
        out_shape=jax.ShapeDtypeStruct((m, d), F32),
        compiler_params=_params("arbitrary"),
        name=name + "_down",
    )(act, w_out, x1, g2, npost)


def _rope_tables(n_tokens):
    rows = n_tokens // GRID_W
    half = HEAD_DIM // 4
    r = jnp.repeat(jnp.arange(rows), GRID_W).astype(F32)
    col = jnp.tile(jnp.arange(GRID_W), rows).astype(F32)
    inv = ROPE_THETA ** (-jnp.arange(half, dtype=F32) / half)
    ar = r[:, None] * inv
    ac = col[:, None] * inv
    cos = jnp.concatenate([jnp.cos(ar), jnp.cos(ar), jnp.cos(ac), jnp.cos(ac)], axis=-1)
    sin = jnp.concatenate([-jnp.sin(ar), jnp.sin(ar), -jnp.sin(ac), jnp.sin(ac)], axis=-1)
    return cos, sin


def _layer(x3, mods, weights, rope_tabs, cache, states, tag):
    batch, seq, d = x3.shape
    m = batch * seq
    x = x3.reshape(m, d)
    (w_in, q_norm, k_norm, lb_f, lb_b, hg_norm, wa, wh, wo, w_ffn_in, w_ffn_out,
     n_pre_mix, n_post_mix, n_pre_ffn, n_post_ffn) = weights
    nb = mods.shape[0]
    sh1, sc1, g1, sh2, sc2, g2 = [mods[:, i].reshape(nb, 1, d) for i in range(6)]
    seq_mod = seq if nb == batch else m

    h = _modnorm(x, n_pre_mix, sc1, sh1, seq_mod)
    q, kv, qh, vh, og, lf_f, k_f, lf_b, k_b, gates = _in_projection(
        h, w_in, batch, seq, rope_tabs, q_norm, k_norm, lb_f, lb_b, tag)

    if cache is None:
        k_bf, v_bf, k_f32, v_f32 = kv
        att = _attention(q, k_bf, v_bf, seq, "attn_" + tag)
    else:
        k_bf, v_bf = kv
        att = _attention(q, jnp.concatenate([k_bf, cache[0]], axis=2),
                         jnp.concatenate([v_bf, cache[1]], axis=2), 256, "attn_" + tag)
        k_f32 = v_f32 = None

    hps = HG_HEADS if seq <= 2 * HG_CHUNK else 1
    if states is None:
        hg, s_f, s_b = _hgrn(qh, k_f, k_b, lf_f, lf_b, vh, og, hg_norm, None, None, hps, "hgrn_" + tag)
    else:
        hg = _hgrn(qh, k_f, k_b, lf_f, lf_b, vh, og, hg_norm, states[0], states[1], hps, "hgrn_" + tag)
        s_f = s_b = None

    x1, h2 = _merge(att.reshape(m, ATT_WIDTH), hg, gates, x, wa, wh, wo,
                    g1, sc2, sh2, n_post_mix, n_pre_ffn, seq_mod, "merge_" + tag)
    y = _ffn(h2, x1, w_ffn_in, w_ffn_out, g2, n_post_ffn, seq_mod, "ffn_" + tag)
    return y.reshape(batch, seq, d), k_f32, v_f32, s_f, s_b


def kernel(x_prompt, x_sample, cache_k, cache_v, state_fwd, state_bwd, c, c_ctx, w_ada, b_ada,
           norm_pre_mix, norm_post_mix, norm_pre_ffn, norm_post_ffn, w_in, q_norm, k_norm,
           lb_fwd, lb_bwd, hg_norm, w_br_att, w_br_hg, w_out, w_ffn_in, w_ffn_out):
    depth = w_in.shape[0]
    assert depth == 1 and lb_fwd.shape[0] == 2
    batch, seq, d = x_prompt.shape
    dec_batch, dec_seq, _ = x_sample.shape
    past = cache_k.shape[2]

    cond = jnp.concatenate([c_ctx[None, :], c, jnp.zeros((8 - 1 - dec_batch, d), F32)], axis=0)
    mods = _ada(cond, w_ada[0], b_ada[0][None, :]).reshape(8, 6, d)

    segments = (("q", OFF_Q, OFF_KV), ("kv", OFF_KV, OFF_QH), ("qh", OFF_QH, OFF_ZF),
                ("zf", OFF_ZF, OFF_ZB), ("zb", OFF_ZB, OFF_VH), ("vh", OFF_VH, OFF_OG),
                ("og", OFF_OG, OFF_GA), ("gates", OFF_GA, OFF_GB + D_MODEL))
    w_in_segments = {name: w_in[0, :, a:b].astype(BF) for name, a, b in segments}
    weights = (
        w_in_segments, q_norm, k_norm, lb_fwd, lb_bwd, hg_norm,
        w_br_att[0].astype(BF), w_br_hg[0].astype(BF), w_out[0].astype(BF),
        w_ffn_in[0].astype(BF), w_ffn_out[0].astype(BF),
        norm_pre_mix, norm_post_mix, norm_pre_ffn, norm_post_ffn,
    )

    y_p, k_c, v_c, s_f, s_b = _layer(x_prompt, mods[0:1], weights, None, None, None, "ctx")

    cache = (
        cache_k[:, 0].transpose(0, 2, 1, 3).astype(BF),
        cache_v[:, 0].transpose(0, 2, 1, 3).astype(BF),
    )
    states = (state_fwd[:, 0], state_bwd[:, 0])
    y_s, _, _, _, _ = _layer(x_sample, mods[1:1 + dec_batch], weights, _rope_tables(dec_seq),
                             cache, states, "lat")

    new_k = k_c.reshape(batch, 1, seq, ATT_KV_HEADS, HEAD_DIM)
    new_v = v_c.reshape(batch, 1, seq, ATT_KV_HEADS, HEAD_DIM)
    return (y_p, y_s, new_k, new_v, s_f[:, None], s_b[:, None])
```

```python
import functools

import jax
import jax.numpy as jnp
from jax import lax
from jax.experimental import pallas as pl
from jax.experimental.pallas import tpu as pltpu

D_MODEL = 2048
HEAD_DIM = 128
ATT_HEADS = 8
ATT_KV_HEADS = 2
GQA_GROUP = ATT_HEADS // ATT_KV_HEADS
ATT_WIDTH = ATT_HEADS * HEAD_DIM
KV_WIDTH = ATT_KV_HEADS * HEAD_DIM
HG_HEADS = 8
HG_WIDTH = HG_HEADS * HEAD_DIM
D_FF = 5632
GRID_W = 64
ROPE_THETA = 10000.0
NORM_EPS = 1e-6
HG_CHUNK = 128
HG_DIAG_LEVEL = 31
HG_GROUP = 4
HG_UNROLL = 1
ATTN_KEY_CHUNK = 256
ATTN_HEADS_PER_PASS = 4
ATTN_ONES_ROWS = 16
LOG2E = 1.4426950408889634

OFF_Q = 0
OFF_KV = ATT_WIDTH
OFF_QH = OFF_KV + 2 * KV_WIDTH
OFF_ZF = OFF_QH + HG_WIDTH
OFF_ZB = OFF_ZF + HG_WIDTH
OFF_VH = OFF_ZB + HG_WIDTH
OFF_OG = OFF_VH + HG_WIDTH
OFF_GA = OFF_OG + HG_WIDTH
OFF_GB = OFF_GA + D_MODEL

BF = jnp.bfloat16
F32 = jnp.float32

VMEM_LIMIT_BYTES = 56 * 1024 * 1024


def _params(*sem):
    return pltpu.CompilerParams(dimension_semantics=sem, vmem_limit_bytes=VMEM_LIMIT_BYTES)


def _rms(x):
    return x * lax.rsqrt(jnp.mean(x * x, axis=-1, keepdims=True) + NORM_EPS)


def _dot(a, b):
    return jnp.dot(a, b, preferred_element_type=F32)


def _dot_nt(a, b):
    return lax.dot_general(a, b, (((1,), (1,)), ((), ())), preferred_element_type=F32)


def _dot_tn(a, b):
    return lax.dot_general(a, b, (((0,), (0,)), ((), ())), preferred_element_type=F32)


def _ada_kernel(c_ref, w_ref, b_ref, o_ref):
    c = c_ref[...]
    s = (c * jax.nn.sigmoid(c)).astype(BF)
    o_ref[...] = _dot(s, w_ref[...].astype(BF)) + b_ref[...]


def _ada(cond, w, b):
    rows, d = cond.shape
    n = w.shape[1]
    tn = 1536
    return pl.pallas_call(
        _ada_kernel,
        grid=(n // tn,),
        in_specs=[
            pl.BlockSpec((rows, d), lambda j: (0, 0)),
            pl.BlockSpec((d, tn), lambda j: (0, j)),
            pl.BlockSpec((1, tn), lambda j: (0, j)),
        ],
        out_specs=pl.BlockSpec((rows, tn), lambda j: (0, j)),
        out_shape=jax.ShapeDtypeStruct((rows, n), F32),
        compiler_params=_params("arbitrary"),
        name="ada_mod",
    )(cond, w, b)


def _modnorm_kernel(x_ref, g_ref, sc_ref, sh_ref, o_ref):
    y = _rms(x_ref[...]) * g_ref[...]
    o_ref[...] = (y * (1.0 + sc_ref[0]) + sh_ref[0]).astype(BF)


def _modnorm(x, g, sc, sh, seq):
    m, d = x.shape
    tm = 512
    tpb = seq // tm
    return pl.pallas_call(
        _modnorm_kernel,
        grid=(m // tm,),
        in_specs=[
            pl.BlockSpec((tm, d), lambda i: (i, 0)),
            pl.BlockSpec((1, d), lambda i: (0, 0)),
            pl.BlockSpec((1, 1, d), lambda i: (i // tpb, 0, 0)),
            pl.BlockSpec((1, 1, d), lambda i: (i // tpb, 0, 0)),
        ],
        out_specs=pl.BlockSpec((tm, d), lambda i: (i, 0)),
        out_shape=jax.ShapeDtypeStruct((m, d), BF),
        compiler_params=_params("arbitrary"),
        name="modnorm",
    )(x, g, sc, sh)


def _head_rms(x):
    ones = jnp.ones((HEAD_DIM, HEAD_DIM), BF)
    ss = _dot((x * x).astype(BF), ones)
    return x * lax.rsqrt(ss * (1.0 / HEAD_DIM) + NORM_EPS)


def _rope(y, cos_ref, sin_ref):
    i = lax.broadcasted_iota(jnp.int32, (HEAD_DIM, HEAD_DIM), 0)
    j = lax.broadcasted_iota(jnp.int32, (HEAD_DIM, HEAD_DIM), 1)
    partner = jnp.where(i == (j ^ 32), 1.0, 0.0).astype(BF)
    return y * cos_ref[...] + _dot(y.astype(BF), partner) * sin_ref[...]


def _store_heads(o_ref, h, y):
    nbb, _, ts, _ = o_ref.shape
    for bl in range(nbb):
        o_ref[bl, h] = y[bl * ts:(bl + 1) * ts].astype(o_ref.dtype)


def _q_epilogue(rope, acc, col0, refs):
    if rope:
        qn_ref, cos_ref, sin_ref, o_ref = refs
    else:
        qn_ref, o_ref = refs
    scale = LOG2E * HEAD_DIM ** -0.5
    for hl in range(acc.shape[1] // HEAD_DIM):
        y = _head_rms(acc[:, hl * HEAD_DIM:(hl + 1) * HEAD_DIM]) * qn_ref[...]
        if rope:
            y = _rope(y, cos_ref, sin_ref)
        _store_heads(o_ref, col0 // HEAD_DIM + hl, y * scale)


def _kv_epilogue(rope, acc, col0, refs):
    if rope:
        kn_ref, cos_ref, sin_ref, kb_ref, vb_ref = refs
    else:
        kn_ref, kb_ref, vb_ref, kf_ref, vf_ref = refs
    is_k = col0 < KV_WIDTH
    for hl in range(acc.shape[1] // HEAD_DIM):
        h = (col0 % KV_WIDTH) // HEAD_DIM + hl
        sl = slice(h * HEAD_DIM, (h + 1) * HEAD_DIM)
        y = acc[:, hl * HEAD_DIM:(hl + 1) * HEAD_DIM]
        if is_k:
            y = _head_rms(y) * kn_ref[...]
        if not rope:
            (kf_ref if is_k else vf_ref)[:, sl] = y
        elif is_k:
            y = _rope(y, cos_ref, sin_ref)
        _store_heads(kb_ref if is_k else vb_ref, h, y)


def _heads_epilogue(act, acc, col0, refs):
    (o_ref,) = refs
    for hl in range(acc.shape[1] // HEAD_DIM):
        y = acc[:, hl * HEAD_DIM:(hl + 1) * HEAD_DIM]
        if act == "silu":
            y = y * jax.nn.sigmoid(y)
        _store_heads(o_ref, col0 // HEAD_DIM + hl, y)


def _forget_epilogue(acc, col0, refs):
    lbp_ref, lf_ref, k_ref = refs
    p = lbp_ref[:, col0:col0 + acc.shape[1]]
    e = jnp.exp(p - jnp.max(p, axis=0, keepdims=True))
    lb = e[0:1] / jnp.sum(e, axis=0, keepdims=True)
    z = acc
    t = jnp.exp(-jnp.abs(z))
    r = 1.0 / (1.0 + t)
    tr = t * r
    pos = z >= 0
    sig_p = jnp.where(pos, r, tr)
    sig_n = jnp.where(pos, tr, r)
    logf = jnp.log2(lb + (1.0 - lb) * sig_p)
    kk = (1.0 - lb) * sig_n
    for hl in range(acc.shape[1] // HEAD_DIM):
        sl = slice(hl * HEAD_DIM, (hl + 1) * HEAD_DIM)
        _store_heads(lf_ref, col0 // HEAD_DIM + hl, logf[:, sl])
        _store_heads(k_ref, col0 // HEAD_DIM + hl, kk[:, sl])


def _sigmoid_epilogue(acc, col0, refs):
    (o_ref,) = refs
    o_ref[:, col0:col0 + acc.shape[1]] = jax.nn.sigmoid(acc).astype(BF)


def _proj_kernel(epilogue, h_ref, w_ref, *refs):
    h = h_ref[...]
    starts = list(range(0, w_ref.shape[1], PROJ_COL_GROUP))

    def product(c0):
        return _dot(h, w_ref[:, c0:c0 + PROJ_COL_GROUP])

    nxt = product(starts[0])
    for i, c0 in enumerate(starts):
        acc = nxt
        if i + 1 < len(starts):
            nxt = product(starts[i + 1])
        epilogue(acc, c0, refs)


def _proj(h, w, tn, epilogue, extra, extra_specs, out_shapes, out_specs, tm, name):
    m, d = h.shape
    ncols = w.shape[1]
    return pl.pallas_call(
        functools.partial(_proj_kernel, epilogue),
        grid=(m // tm, ncols // tn),
        in_specs=[
            pl.BlockSpec((tm, d), lambda i, j: (i, 0)),
            pl.BlockSpec((d, tn), lambda i, j: (0, j)),
        ] + extra_specs,
        out_specs=out_specs,
        out_shape=out_shapes,
        compiler_params=_params("arbitrary", "arbitrary"),
        name=name,
    )(h, w, *extra)


PROJ_TOKENS = 1024
PROJ_COL_GROUP = 256


def _in_projection(h, ws, batch, seq, rope_tabs, q_norm, k_norm, lb_f, lb_b, tag):
    m = h.shape[0]
    tm = PROJ_TOKENS
    rope = rope_tabs is not None
    if seq >= tm:
        tpb = seq // tm
        head_block = (1, tm)

        def head_index(i, j):
            return (i // tpb, 0, i % tpb, 0)
    else:
        tpb = 1
        head_block = (tm // seq, seq)

        def head_index(i, j):
            return (i, 0, 0, 0)

    def head_spec(nh):
        return pl.BlockSpec((head_block[0], nh, head_block[1], HEAD_DIM), head_index)

    def head_shape(nh, dt=BF):
        return jax.ShapeDtypeStruct((batch, nh, seq, HEAD_DIM), dt)

    vec_spec = pl.BlockSpec((1, HEAD_DIM), lambda i, j: (0, 0))
    tab_spec = pl.BlockSpec((tm, HEAD_DIM), lambda i, j: (i % tpb, 0))
    rope_in = list(rope_tabs) if rope else []
    rope_specs = [tab_spec, tab_spec] if rope else []

    q = _proj(h, ws["q"], ATT_WIDTH, functools.partial(_q_epilogue, rope),
              [q_norm] + rope_in, [vec_spec] + rope_specs,
              head_shape(ATT_HEADS), head_spec(ATT_HEADS), tm, "proj_q_" + tag)

    kv_shapes = [head_shape(ATT_KV_HEADS), head_shape(ATT_KV_HEADS)]
    kv_specs = [head_spec(ATT_KV_HEADS), head_spec(ATT_KV_HEADS)]
    if not rope:
        tok_spec = pl.BlockSpec((tm, KV_WIDTH), lambda i, j: (i, 0))
        kv_shapes += [jax.ShapeDtypeStruct((m, KV_WIDTH), F32)] * 2
        kv_specs += [tok_spec, tok_spec]
    kv = _proj(h, ws["kv"], 2 * KV_WIDTH, functools.partial(_kv_epilogue, rope),
               [k_norm] + rope_in, [vec_spec] + rope_specs,
               kv_shapes, kv_specs, tm, "proj_kv_" + tag)

    def heads(seg, act):
        return _proj(h, ws[seg], HG_WIDTH, functools.partial(_heads_epilogue, act),
                     [], [], head_shape(HG_HEADS), head_spec(HG_HEADS), tm, "proj_" + seg + "_" + tag)

    qh = heads("qh", "silu")
    vh = heads("vh", "none")
    og = heads("og", "silu")

    def forget(seg, lbp):
        return _proj(h, ws[seg], HG_WIDTH, _forget_epilogue,
                     [lbp], [pl.BlockSpec(lbp.shape, lambda i, j: (0, 0))],
                     [head_shape(HG_HEADS, F32), head_shape(HG_HEADS)],
                     [head_spec(HG_HEADS), head_spec(HG_HEADS)], tm, "proj_" + seg + "_" + tag)

    lf_f, k_f = forget("zf", lb_f)
    lf_b, k_b = forget("zb", lb_b)

    tn = 1024
    gates = _proj(h, ws["gates"], tn, _sigmoid_epilogue, [], [],
                  jax.ShapeDtypeStruct((m, 2 * D_MODEL), BF),
                  pl.BlockSpec((tm, tn), lambda i, j: (i, j)), tm, "proj_gates_" + tag)
    return q, kv, qh, vh, og, lf_f, k_f, lf_b, k_b, gates


def _attn_kernel(q_ref, k_ref, v_ref, o_ref, vt_scr):
    @pl.when(pl.program_id(2) == 0)
    def _():
        vt_scr[0:HEAD_DIM, :] = v_ref[0, 0].T
        vt_scr[HEAD_DIM:, :] = jnp.ones((ATTN_ONES_ROWS, vt_scr.shape[1]), BF)

    tq = q_ref.shape[2]
    keys = k_ref.shape[2]
    n_chunks = keys // ATTN_KEY_CHUNK
    n_pass = GQA_GROUP // ATTN_HEADS_PER_PASS
    qs = [jnp.concatenate([q_ref[0, g] for g in range(ps * ATTN_HEADS_PER_PASS,
                                                       (ps + 1) * ATTN_HEADS_PER_PASS)], axis=0)
          for ps in range(n_pass)]
    items = [(c, ps) for c in range(n_chunks) for ps in range(n_pass)]

    def scores(item):
        c, ps = item
        return _dot_nt(k_ref[0, 0, c * ATTN_KEY_CHUNK:(c + 1) * ATTN_KEY_CHUNK, :], qs[ps])

    state = [None] * n_pass
    s_next = scores(items[0])
    for i, (c, ps) in enumerate(items):
        ks = slice(c * ATTN_KEY_CHUNK, (c + 1) * ATTN_KEY_CHUNK)
        s = s_next
        if i + 1 < len(items):
            s_next = scores(items[i + 1])
        smax = jnp.max(s, axis=0, keepdims=True)
        if c == 0:
            state[ps] = (smax, _dot(vt_scr[:, ks], jnp.exp2(s - smax).astype(BF)))
        else:
            m, acc = state[ps]
            m_new = jnp.maximum(m, smax)
            p = jnp.exp2(s - m_new).astype(BF)
            state[ps] = (m_new, acc * jnp.exp2(m - m_new) + _dot(vt_scr[:, ks], p))
    for ps in range(n_pass):
        acc = state[ps][1]
        o = acc[0:HEAD_DIM] / acc[HEAD_DIM:HEAD_DIM + 1]
        for i in range(ATTN_HEADS_PER_PASS):
            g = ps * ATTN_HEADS_PER_PASS + i
            o_ref[0, :, g * HEAD_DIM:(g + 1) * HEAD_DIM] = o[:, i * tq:(i + 1) * tq].T.astype(BF)


def _attention(q, k, v, tq, name):
    batch, _, seq, _ = q.shape
    keys = k.shape[2]
    kv_spec = pl.BlockSpec((1, 1, keys, HEAD_DIM), lambda b, kv, t: (b, kv, 0, 0))
    return pl.pallas_call(
        _attn_kernel,
        grid=(batch, ATT_KV_HEADS, seq // tq),
        in_specs=[pl.BlockSpec((1, GQA_GROUP, tq, HEAD_DIM), lambda b, kv, t: (b, kv, t, 0)),
                  kv_spec, kv_spec],
        out_specs=pl.BlockSpec((1, tq, GQA_GROUP * HEAD_DIM), lambda b, kv, t: (b, t, kv)),
        out_shape=jax.ShapeDtypeStruct((batch, seq, ATT_WIDTH), BF),
        scratch_shapes=[pltpu.VMEM((HEAD_DIM + ATTN_ONES_ROWS, keys), BF)],
        compiler_params=_params("arbitrary", "arbitrary", "arbitrary"),
        name=name,
    )(q, k, v)


def _split3(x):
    hi = x.astype(BF)
    r1 = x - hi.astype(F32)
    mid = r1.astype(BF)
    lo = (r1 - mid.astype(F32)).astype(BF)
    return hi, mid, lo


def _block_reference(b3, hb, rev):
    n8 = b3.shape[0]
    if hb < 8:
        def row(r):
            return jnp.broadcast_to(b3[:, r:r + 1, :], b3.shape)
        if hb == 4:
            return row(4 if rev else 3)
        assert hb == 2
        sub = lax.broadcasted_iota(jnp.int32, b3.shape, 1)
        r0, r1 = (2, 6) if rev else (1, 5)
        return jnp.where(sub < 4, row(r0), row(r1))
    g = hb // 8
    nblk = n8 // (2 * g)
    edge = (b3[:, 0:1, :] if rev else b3[:, 7:8, :]).reshape(nblk, 2 * g, 1, b3.shape[2])
    pick = edge[:, g:g + 1] if rev else edge[:, g - 1:g]
    return jnp.broadcast_to(pick, (nblk, 2 * g, 8, b3.shape[2])).reshape(b3.shape)


def _chunk_cumsum(lf, tri_b):
    hi, mid, lo = _split3(lf)
    return _dot(tri_b, hi) + _dot(tri_b, mid) + _dot(tri_b, lo)


def _rows8(x, r):
    return x[r * 8:(r + 1) * 8]


def _hg_products(q, k, lf, b, vt, rev):
    c = q.shape[0]
    n8 = c // 8
    b3 = b.reshape(n8, 8, HEAD_DIM)
    q32 = q.astype(F32)
    k32 = k.astype(F32)
    sub = lax.broadcasted_iota(jnp.int32, (c, HEAD_DIM), 0)
    qbit = 0 if rev else 1
    products = [(HG_DIAG_LEVEL, list(range(n8)), _dot_nt(q, k))]
    hb, j = 1, 0
    while hb < c:
        if hb < 8:
            is_query = ((sub & hb) == 0) if rev else ((sub & hb) != 0)
            if hb == 1:
                arg = jnp.where(is_query, lf, 0.0)
            else:
                arg = -jnp.abs(b - _block_reference(b3, hb, rev).reshape(c, HEAD_DIM))
            x = (jnp.where(is_query, q32, k32) * jnp.exp2(arg)).astype(BF)
            prod = _dot_nt(x, x)
            q_groups = list(range(n8))
        else:
            q_groups = [r for r in range(n8) if ((r * 8 // hb) & 1) == qbit]
            ref = _block_reference(b3, hb, rev).reshape(c, HEAD_DIM)
            mixed = jnp.concatenate([_rows8(q32 if r in q_groups else k32, r) for r in range(n8)], axis=0)
            arg = jnp.concatenate([_rows8(b, r) - _rows8(ref, r) if r in q_groups
                                   else _rows8(ref, r) - _rows8(b, r) for r in range(n8)], axis=0)
            y = mixed * jnp.exp2(arg)
            lhs = jnp.concatenate([_rows8(y, r) for r in q_groups], axis=0).astype(BF)
            prod = _dot_nt(lhs, y.astype(BF))
        products.append((j, q_groups, prod))
        hb, j = 2 * hb, j + 1
    btot = b[0:1, :] if rev else b[c - 1:c, :]
    qs = (q32 * jnp.exp2(b)).astype(BF)
    ks = (k32 * jnp.exp2(btot - b)).astype(BF)
    return products, qs, jnp.exp2(btot), _dot(vt, ks)


def _hg_finish(parts, vt, st, level):
    products, qs, decay, contribution = parts
    a_rows = [None] * (level.shape[0] // 8)
    for code, q_groups, prod in products:
        for i, r in enumerate(q_groups):
            keep = _rows8(level, r) == code
            a_rows[r] = jnp.where(keep, _rows8(prod, i), 0.0 if a_rows[r] is None else a_rows[r])
    a = jnp.concatenate(a_rows, axis=0)
    o = _dot_nt(jnp.concatenate([a.astype(BF), qs], axis=1),
                jnp.concatenate([vt, st.astype(BF)], axis=1))
    return o, st * decay + contribution


def _pair_levels(c, rev):
    t = lax.broadcasted_iota(jnp.int32, (c, c), 0)
    s = lax.broadcasted_iota(jnp.int32, (c, c), 1)
    x = t ^ s
    lvl = jnp.zeros((c, c), jnp.int32)
    p = 2
    while p < c:
        lvl = lvl + jnp.where(x >= p, 1, 0)
        p *= 2
    valid = (s > t) if rev else (s < t)
    return jnp.where(valid, lvl, jnp.where(s == t, HG_DIAG_LEVEL, -1))


def _hgrn_kernel(seq, heads, zero_init, q_ref, kf_ref, kb_ref, lf_ref, lb_ref, v_ref, og_ref, *refs):
    if zero_init:
        hn_ref, o_ref, sf_ref, sb_ref, vt_scr, bf_scr, bb_scr, of_scr, ob_scr = refs
    else:
        s0f_ref, s0b_ref, hn_ref, o_ref, vt_scr, bf_scr, bb_scr, of_scr, ob_scr = refs
    c = HG_CHUNK
    n = seq // c
    t = lax.broadcasted_iota(jnp.int32, (c, c), 0)
    s = lax.broadcasted_iota(jnp.int32, (c, c), 1)
    tril_b = jnp.where(s <= t, 1.0, 0.0).astype(BF)
    triu_b = jnp.where(s >= t, 1.0, 0.0).astype(BF)
    level_f = _pair_levels(c, False)
    level_b = _pair_levels(c, True)

    def rows(i):
        return pl.ds(pl.multiple_of(i * c, c), c)

    def head_body(h, _):
        def prepare(i, _):
            r = rows(i)
            vt_scr[i] = v_ref[0, h, r, :].T
            bf_scr[r, :] = _chunk_cumsum(lf_ref[0, h, r, :], tril_b)
            bb_scr[r, :] = _chunk_cumsum(lb_ref[0, h, r, :], triu_b)
            return 0

        lax.fori_loop(0, n, prepare, 0, unroll=2)

        group = min(n, HG_GROUP)

        def body(g, carry):
            st_f, st_b = carry
            work = []
            for u in range(group):
                i = g * group + u
                jf, jb = i, n - 1 - i
                rf, rb = rows(jf), rows(jb)
                work.append((rf, jf, _hg_products(q_ref[0, h, rf, :], kf_ref[0, h, rf, :],
                                                  lf_ref[0, h, rf, :], bf_scr[rf, :], vt_scr[jf], False),
                             rb, jb, _hg_products(q_ref[0, h, rb, :], kb_ref[0, h, rb, :],
                                                  lb_ref[0, h, rb, :], bb_scr[rb, :], vt_scr[jb], True)))
            for rf, jf, parts_f, rb, jb, parts_b in work:
                o_f, st_f = _hg_finish(parts_f, vt_scr[jf], st_f, level_f)
                of_scr[rf, :] = o_f
                o_b, st_b = _hg_finish(parts_b, vt_scr[jb], st_b, level_b)
                ob_scr[rb, :] = o_b
            return st_f, st_b

        if zero_init:
            st0 = (jnp.zeros((HEAD_DIM, HEAD_DIM), F32), jnp.zeros((HEAD_DIM, HEAD_DIM), F32))
        else:
            st0 = (s0f_ref[0, h].T, s0b_ref[0, h].T)
        st_f, st_b = lax.fori_loop(0, n // group, body, st0, unroll=min(n // group, HG_UNROLL))
        o = of_scr[...] + ob_scr[...]
        y = _rms(o) * hn_ref[...] * og_ref[0, h].astype(F32)
        o_ref[0, h] = y.astype(BF)
        if zero_init:
            sf_ref[0, h] = st_f.T
            sb_ref[0, h] = st_b.T
        return 0

    lax.fori_loop(0, heads, head_body, 0)


def _hgrn(qh, k_f, k_b, lf_f, lf_b, vh, og, hg_norm, s0f, s0b, heads_per_step, name):
    batch, heads, seq, hd = qh.shape
    hps = heads_per_step
    zero_init = s0f is None
    head_spec = pl.BlockSpec((1, hps, seq, hd), lambda b, h: (b, h, 0, 0))
    state_spec = pl.BlockSpec((1, hps, hd, hd), lambda b, h: (b, h, 0, 0))
    vec_spec = pl.BlockSpec((1, hd), lambda b, h: (0, 0))
    o_shape = jax.ShapeDtypeStruct((batch, heads, seq, hd), BF)
    ins = [qh, k_f, k_b, lf_f, lf_b, vh, og]
    in_specs = [head_spec] * 7
    if zero_init:
        state_shape = jax.ShapeDtypeStruct((batch, heads, hd, hd), F32)
        out_shape = [o_shape, state_shape, state_shape]
        out_specs = [head_spec, state_spec, state_spec]
    else:
        ins += [s0f, s0b]
        in_specs += [state_spec, state_spec]
        out_shape = o_shape
        out_specs = head_spec
    ins.append(hg_norm)
    in_specs.append(vec_spec)
    return pl.pallas_call(
        functools.partial(_hgrn_kernel, seq, hps, zero_init),
        grid=(batch, heads // hps),
        in_specs=in_specs,
        out_specs=out_specs,
        out_shape=out_shape,
        scratch_shapes=[pltpu.VMEM((seq // HG_CHUNK, hd, HG_CHUNK), BF)]
        + [pltpu.VMEM((seq, hd), F32)] * 4,
        compiler_params=_params("arbitrary", "arbitrary"),
        name=name,
    )(*ins)


def _merge_kernel(att_ref, hg_ref, ga_ref, gb_ref, x_ref, wa_ref, wh_ref, wo_ref,
                  g1_ref, sc2_ref, sh2_ref, npost_ref, npre_ref, x1_ref, h2_ref):
    a = _dot(att_ref[...], wa_ref[...])
    hg = jnp.concatenate([hg_ref[0, h] for h in range(HG_HEADS)], axis=1)
    b = _dot(hg, wh_ref[...])
    m = (ga_ref[...].astype(F32) * a + gb_ref[...].astype(F32) * b).astype(BF)
    mo = _dot(m, wo_ref[...])
    x1 = x_ref[...] + g1_ref[0] * (_rms(mo) * npost_ref[...])
    x1_ref[...] = x1
    h2 = _rms(x1) * npre_ref[...]
    h2_ref[...] = (h2 * (1.0 + sc2_ref[0]) + sh2_ref[0]).astype(BF)


def _merge(att, hg, gates, x, wa, wh, wo, g1, sc2, sh2, npost, npre, seq, name):
    m, d = x.shape
    tm = 256
    tpb = seq // tm
    w_att = att.shape[1]
    hg_tpb = hg.shape[2] // tm
    hg_spec = pl.BlockSpec((1, HG_HEADS, tm, HEAD_DIM), lambda i: (i // hg_tpb, 0, i % hg_tpb, 0))

    def const(shape):
        return pl.BlockSpec(shape, lambda i: (0, 0), pipeline_mode=pl.Buffered(1))

    mod_spec = pl.BlockSpec((1, 1, d), lambda i: (i // tpb, 0, 0))
    vec_spec = pl.BlockSpec((1, d), lambda i: (0, 0))
    return pl.pallas_call(
        _merge_kernel,
        grid=(m // tm,),
        in_specs=[
            pl.BlockSpec((tm, w_att), lambda i: (i, 0)),
            hg_spec,
            pl.BlockSpec((tm, d), lambda i: (i, 0)),
            pl.BlockSpec((tm, d), lambda i: (i, 1)),
            pl.BlockSpec((tm, d), lambda i: (i, 0)),
            const(wa.shape), const(wh.shape), const(wo.shape),
            mod_spec, mod_spec, mod_spec, vec_spec, vec_spec,
        ],
        out_specs=[pl.BlockSpec((tm, d), lambda i: (i, 0)), pl.BlockSpec((tm, d), lambda i: (i, 0))],
        out_shape=[jax.ShapeDtypeStruct((m, d), F32), jax.ShapeDtypeStruct((m, d), BF)],
        compiler_params=_params("arbitrary"),
        name=name,
    )(att, hg, gates, gates, x, wa, wh, wo, g1, sc2, sh2, npost, npre)


def _ffn_up_kernel(h_ref, wg_ref, wu_ref, a_ref):
    h = h_ref[...]
    g = _dot(h, wg_ref[...])
    u = _dot(h, wu_ref[...])
    a_ref[...] = (g * jax.nn.sigmoid(g) * u).astype(BF)


def _ffn_down_kernel(a_ref, wo_ref, x1_ref, g2_ref, npost_ref, o_ref):
    y = _rms(_dot(a_ref[...], wo_ref[...])) * npost_ref[...]
    o_ref[...] = x1_ref[...] + g2_ref[0] * y


def _ffn(h2, x1, w_in, w_out, g2, npost, seq, name):
    m, d = x1.shape
    tm, tf = PROJ_TOKENS, 512
    nf = D_FF // tf
    act = pl.pallas_call(
        _ffn_up_kernel,
        grid=(m // tm, nf),
        in_specs=[
            pl.BlockSpec((tm, d), lambda i, f: (i, 0)),
            pl.BlockSpec((d, tf), lambda i, f: (0, f)),
            pl.BlockSpec((d, tf), lambda i, f: (0, nf + f)),
        ],
        out_specs=pl.BlockSpec((tm, tf), lambda i, f: (i, f)),
        out_shape=jax.ShapeDtypeStruct((m, D_FF), BF),
        compiler_params=_params("arbitrary", "arbitrary"),
        name=name + "_up",
    )(h2, w_in, w_in)
    td = 256
    tpb = seq // td
    return pl.pallas_call(
        _ffn_down_kernel,
        grid=(m // td,),
        in_specs=[
            pl.BlockSpec((td, D_FF), lambda i: (i, 0)),
            pl.BlockSpec((D_FF, d), lambda i: (0, 0), pipeline_mode=pl.Buffered(1)),
            pl.BlockSpec((td, d), lambda i: (i, 0)),
            pl.BlockSpec((1, 1, d), lambda i: (i // tpb, 0, 0)),
            pl.BlockSpec((1, d), lambda i: (0, 0)),
        ],
        out_specs=pl.BlockSpec((td, d), lambda i: (i, 0)),
        out_shape=jax.ShapeDtypeStruct((m, d), F32),
        compiler_params=_params("arbitrary"),
        name=name + "_down",
    )(act, w_out, x1, g2, npost)


def _rope_tables(n_tokens):
    rows = n_tokens // GRID_W
    half = HEAD_DIM // 4
    r = jnp.repeat(jnp.arange(rows), GRID_W).astype(F32)
    col = jnp.tile(jnp.arange(GRID_W), rows).astype(F32)
    inv = ROPE_THETA ** (-jnp.arange(half, dtype=F32) / half)
    ar = r[:, None] * inv
    ac = col[:, None] * inv
    cos = jnp.concatenate([jnp.cos(ar), jnp.cos(ar), jnp.cos(ac), jnp.cos(ac)], axis=-1)
    sin = jnp.concatenate([-jnp.sin(ar), jnp.sin(ar), -jnp.sin(ac), jnp.sin(ac)], axis=-1)
    return cos, sin


def _layer(x3, mods, weights, rope_tabs, cache, states, tag):
    batch, seq, d = x3.shape
    m = batch * seq
    x = x3.reshape(m, d)
    (w_in, q_norm, k_norm, lb_f, lb_b, hg_norm, wa, wh, wo, w_ffn_in, w_ffn_out,
     n_pre_mix, n_post_mix, n_pre_ffn, n_post_ffn) = weights
    nb = mods.shape[0]
    sh1, sc1, g1, sh2, sc2, g2 = [mods[:, i].reshape(nb, 1, d) for i in range(6)]
    seq_mod = seq if nb == batch else m

    h = _modnorm(x, n_pre_mix, sc1, sh1, seq_mod)
    q, kv, qh, vh, og, lf_f, k_f, lf_b, k_b, gates = _in_projection(
        h, w_in, batch, seq, rope_tabs, q_norm, k_norm, lb_f, lb_b, tag)

    if cache is None:
        k_bf, v_bf, k_f32, v_f32 = kv
        att = _attention(q, k_bf, v_bf, seq, "attn_" + tag)
    else:
        k_bf, v_bf = kv
        att = _attention(q, jnp.concatenate([k_bf, cache[0]], axis=2),
                         jnp.concatenate([v_bf, cache[1]], axis=2), 256, "attn_" + tag)
        k_f32 = v_f32 = None

    hps = HG_HEADS if seq <= 2 * HG_CHUNK else 1
    if states is None:
        hg, s_f, s_b = _hgrn(qh, k_f, k_b, lf_f, lf_b, vh, og, hg_norm, None, None, hps, "hgrn_" + tag)
    else:
        hg = _hgrn(qh, k_f, k_b, lf_f, lf_b, vh, og, hg_norm, states[0], states[1], hps, "hgrn_" + tag)
        s_f = s_b = None

    x1, h2 = _merge(att.reshape(m, ATT_WIDTH), hg, gates, x, wa, wh, wo,
                    g1, sc2, sh2, n_post_mix, n_pre_ffn, seq_mod, "merge_" + tag)
    y = _ffn(h2, x1, w_ffn_in, w_ffn_out, g2, n_post_ffn, seq_mod, "ffn_" + tag)
    return y.reshape(batch, seq, d), k_f32, v_f32, s_f, s_b


def kernel(x_prompt, x_sample, cache_k, cache_v, state_fwd, state_bwd, c, c_ctx, w_ada, b_ada,
           norm_pre_mix, norm_post_mix, norm_pre_ffn, norm_post_ffn, w_in, q_norm, k_norm,
           lb_fwd, lb_bwd, hg_norm, w_br_att, w_br_hg, w_out, w_ffn_in, w_ffn_out):
    depth = w_in.shape[0]
    assert depth == 1 and lb_fwd.shape[0] == 2
    batch, seq, d = x_prompt.shape
    dec_batch, dec_seq, _ = x_sample.shape
    past = cache_k.shape[2]

    cond = jnp.concatenate([c_ctx[None, :], c, jnp.zeros((8 - 1 - dec_batch, d), F32)], axis=0)
    mods = _ada(cond, w_ada[0], b_ada[0][None, :]).reshape(8, 6, d)

    segments = (("q", OFF_Q, OFF_KV), ("kv", OFF_KV, OFF_QH), ("qh", OFF_QH, OFF_ZF),
                ("zf", OFF_ZF, OFF_ZB), ("zb", OFF_ZB, OFF_VH), ("vh", OFF_VH, OFF_OG),
                ("og", OFF_OG, OFF_GA), ("gates", OFF_GA, OFF_GB + D_MODEL))
    w_in_segments = {name: w_in[0, :, a:b].astype(BF) for name, a, b in segments}
    weights = (
        w_in_segments, q_norm, k_norm, lb_fwd, lb_bwd, hg_norm,
        w_br_att[0].astype(BF), w_br_hg[0].astype(BF), w_out[0].astype(BF),
        w_ffn_in[0].astype(BF), w_ffn_out[0].astype(BF),
        norm_pre_mix, norm_post_mix, norm_pre_ffn, norm_post_ffn,
    )

    y_p, k_c, v_c, s_f, s_b = _layer(x_prompt, mods[0:1], weights, None, None, None, "ctx")

    cache = (
        cache_k[:, 0].transpose(0, 2, 1, 3).astype(BF),
        cache_v[:, 0].transpose(0, 2, 1, 3).astype(BF),
    )
    states = (state_fwd[:, 0], state_bwd[:, 0])
    y_s, _, _, _, _ = _layer(x_sample, mods[1:1 + dec_batch], weights, _rope_tables(dec_seq),
                             cache, states, "lat")

    new_k = k_c.reshape(batch, 1, seq, ATT_KV_HEADS, HEAD_DIM)
    new_v = v_c.reshape(batch, 1, seq, ATT_KV_HEADS, HEAD_DIM)
    return (y_p, y_s, new_k, new_v, s_f[:, None], s_b[:, None])
```

```python
import functools

import jax
import jax.numpy as jnp
from jax import lax
from jax.experimental import pallas as pl
from jax.experimental.pallas import tpu as pltpu

D_MODEL = 2048
HEAD_DIM = 128
ATT_HEADS = 8
ATT_KV_HEADS = 2
GQA_GROUP = ATT_HEADS // ATT_KV_HEADS
ATT_WIDTH = ATT_HEADS * HEAD_DIM
KV_WIDTH = ATT_KV_HEADS * HEAD_DIM
HG_HEADS = 8
HG_WIDTH = HG_HEADS * HEAD_DIM
D_FF = 5632
GRID_W = 64
ROPE_THETA = 10000.0
NORM_EPS = 1e-6
HG_CHUNK = 128
HG_DIAG_LEVEL = 31
HG_GROUP = 4
HG_UNROLL = 1
ATTN_KEY_CHUNK = 256
ATTN_HEADS_PER_PASS = 4
ATTN_ONES_ROWS = 16
ATTN_LOOKAHEAD = 2
LOG2E = 1.4426950408889634

OFF_Q = 0
OFF_KV = ATT_WIDTH
OFF_QH = OFF_KV + 2 * KV_WIDTH
OFF_ZF = OFF_QH + HG_WIDTH
OFF_ZB = OFF_ZF + HG_WIDTH
OFF_VH = OFF_ZB + HG_WIDTH
OFF_OG = OFF_VH + HG_WIDTH
OFF_GA = OFF_OG + HG_WIDTH
OFF_GB = OFF_GA + D_MODEL

BF = jnp.bfloat16
F32 = jnp.float32

VMEM_LIMIT_BYTES = 56 * 1024 * 1024


def _params(*sem):
    return pltpu.CompilerParams(dimension_semantics=sem, vmem_limit_bytes=VMEM_LIMIT_BYTES)


def _rms(x):
    return x * lax.rsqrt(jnp.mean(x * x, axis=-1, keepdims=True) + NORM_EPS)


def _dot(a, b):
    return jnp.dot(a, b, preferred_element_type=F32)


def _dot_nt(a, b):
    return lax.dot_general(a, b, (((1,), (1,)), ((), ())), preferred_element_type=F32)


def _dot_tn(a, b):
    return lax.dot_general(a, b, (((0,), (0,)), ((), ())), preferred_element_type=F32)


def _ada_kernel(c_ref, w_ref, b_ref, o_ref):
    c = c_ref[...]
    s = (c * jax.nn.sigmoid(c)).astype(BF)
    o_ref[...] = _dot(s, w_ref[...].astype(BF)) + b_ref[...]


def _ada(cond, w, b):
    rows, d = cond.shape
    n = w.shape[1]
    tn = 1536
    return pl.pallas_call(
        _ada_kernel,
        grid=(n // tn,),
        in_specs=[
            pl.BlockSpec((rows, d), lambda j: (0, 0)),
            pl.BlockSpec((d, tn), lambda j: (0, j)),
            pl.BlockSpec((1, tn), lambda j: (0, j)),
        ],
        out_specs=pl.BlockSpec((rows, tn), lambda j: (0, j)),
        out_shape=jax.ShapeDtypeStruct((rows, n), F32),
        compiler_params=_params("arbitrary"),
        name="ada_mod",
    )(cond, w, b)


def _modnorm_kernel(x_ref, g_ref, sc_ref, sh_ref, o_ref):
    y = _rms(x_ref[...]) * g_ref[...]
    o_ref[...] = (y * (1.0 + sc_ref[0]) + sh_ref[0]).astype(BF)


def _modnorm(x, g, sc, sh, seq):
    m, d = x.shape
    tm = 512
    tpb = seq // tm
    return pl.pallas_call(
        _modnorm_kernel,
        grid=(m // tm,),
        in_specs=[
            pl.BlockSpec((tm, d), lambda i: (i, 0)),
            pl.BlockSpec((1, d), lambda i: (0, 0)),
            pl.BlockSpec((1, 1, d), lambda i: (i // tpb, 0, 0)),
            pl.BlockSpec((1, 1, d), lambda i: (i // tpb, 0, 0)),
        ],
        out_specs=pl.BlockSpec((tm, d), lambda i: (i, 0)),
        out_shape=jax.ShapeDtypeStruct((m, d), BF),
        compiler_params=_params("arbitrary"),
        name="modnorm",
    )(x, g, sc, sh)


def _head_rms(x):
    ones = jnp.ones((HEAD_DIM, HEAD_DIM), BF)
    ss = _dot((x * x).astype(BF), ones)
    return x * lax.rsqrt(ss * (1.0 / HEAD_DIM) + NORM_EPS)


def _rope(y, cos_ref, sin_ref):
    i = lax.broadcasted_iota(jnp.int32, (HEAD_DIM, HEAD_DIM), 0)
    j = lax.broadcasted_iota(jnp.int32, (HEAD_DIM, HEAD_DIM), 1)
    partner = jnp.where(i == (j ^ 32), 1.0, 0.0).astype(BF)
    return y * cos_ref[...] + _dot(y.astype(BF), partner) * sin_ref[...]


def _store_heads(o_ref, h, y):
    nbb, _, ts, _ = o_ref.shape
    for bl in range(nbb):
        o_ref[bl, h] = y[bl * ts:(bl + 1) * ts].astype(o_ref.dtype)


def _q_epilogue(rope, acc, col0, refs):
    if rope:
        qn_ref, cos_ref, sin_ref, o_ref = refs
    else:
        qn_ref, o_ref = refs
    scale = LOG2E * HEAD_DIM ** -0.5
    for hl in range(acc.shape[1] // HEAD_DIM):
        y = _head_rms(acc[:, hl * HEAD_DIM:(hl + 1) * HEAD_DIM]) * qn_ref[...]
        if rope:
            y = _rope(y, cos_ref, sin_ref)
        _store_heads(o_ref, col0 // HEAD_DIM + hl, y * scale)


def _kv_epilogue(rope, acc, col0, refs):
    if rope:
        kn_ref, cos_ref, sin_ref, kb_ref, vb_ref = refs
    else:
        kn_ref, kb_ref, vb_ref, kf_ref, vf_ref = refs
    is_k = col0 < KV_WIDTH
    for hl in range(acc.shape[1] // HEAD_DIM):
        h = (col0 % KV_WIDTH) // HEAD_DIM + hl
        sl = slice(h * HEAD_DIM, (h + 1) * HEAD_DIM)
        y = acc[:, hl * HEAD_DIM:(hl + 1) * HEAD_DIM]
        if is_k:
            y = _head_rms(y) * kn_ref[...]
        if not rope:
            (kf_ref if is_k else vf_ref)[:, sl] = y
        elif is_k:
            y = _rope(y, cos_ref, sin_ref)
        _store_heads(kb_ref if is_k else vb_ref, h, y)


def _heads_epilogue(act, acc, col0, refs):
    (o_ref,) = refs
    for hl in range(acc.shape[1] // HEAD_DIM):
        y = acc[:, hl * HEAD_DIM:(hl + 1) * HEAD_DIM]
        if act == "silu":
            y = y * jax.nn.sigmoid(y)
        _store_heads(o_ref, col0 // HEAD_DIM + hl, y)


def _forget_epilogue(acc, col0, refs):
    lbp_ref, lf_ref, k_ref = refs
    p = lbp_ref[:, col0:col0 + acc.shape[1]]
    e = jnp.exp(p - jnp.max(p, axis=0, keepdims=True))
    lb = e[0:1] / jnp.sum(e, axis=0, keepdims=True)
    z = acc
    t = jnp.exp(-jnp.abs(z))
    r = 1.0 / (1.0 + t)
    tr = t * r
    pos = z >= 0
    sig_p = jnp.where(pos, r, tr)
    sig_n = jnp.where(pos, tr, r)
    logf = jnp.log2(lb + (1.0 - lb) * sig_p)
    kk = (1.0 - lb) * sig_n
    for hl in range(acc.shape[1] // HEAD_DIM):
        sl = slice(hl * HEAD_DIM, (hl + 1) * HEAD_DIM)
        _store_heads(lf_ref, col0 // HEAD_DIM + hl, logf[:, sl])
        _store_heads(k_ref, col0 // HEAD_DIM + hl, kk[:, sl])


def _sigmoid_epilogue(acc, col0, refs):
    (o_ref,) = refs
    o_ref[:, col0:col0 + acc.shape[1]] = jax.nn.sigmoid(acc).astype(BF)


def _proj_kernel(epilogue, h_ref, w_ref, *refs):
    h = h_ref[...]
    starts = list(range(0, w_ref.shape[1], PROJ_COL_GROUP))

    def product(c0):
        return _dot(h, w_ref[:, c0:c0 + PROJ_COL_GROUP])

    nxt = product(starts[0])
    for i, c0 in enumerate(starts):
        acc = nxt
        if i + 1 < len(starts):
            nxt = product(starts[i + 1])
        epilogue(acc, c0, refs)


def _proj(h, w, tn, epilogue, extra, extra_specs, out_shapes, out_specs, tm, name):
    m, d = h.shape
    ncols = w.shape[1]
    return pl.pallas_call(
        functools.partial(_proj_kernel, epilogue),
        grid=(m // tm, ncols // tn),
        in_specs=[
            pl.BlockSpec((tm, d), lambda i, j: (i, 0)),
            pl.BlockSpec((d, tn), lambda i, j: (0, j)),
        ] + extra_specs,
        out_specs=out_specs,
        out_shape=out_shapes,
        compiler_params=_params("arbitrary", "arbitrary"),
        name=name,
    )(h, w, *extra)


PROJ_TOKENS = 1024
PROJ_COL_GROUP = 256


def _in_projection(h, ws, batch, seq, rope_tabs, q_norm, k_norm, lb_f, lb_b, tag):
    m = h.shape[0]
    tm = PROJ_TOKENS
    rope = rope_tabs is not None
    if seq >= tm:
        tpb = seq // tm
        head_block = (1, tm)

        def head_index(i, j):
            return (i // tpb, 0, i % tpb, 0)
    else:
        tpb = 1
        head_block = (tm // seq, seq)

        def head_index(i, j):
            return (i, 0, 0, 0)

    def head_spec(nh):
        return pl.BlockSpec((head_block[0], nh, head_block[1], HEAD_DIM), head_index)

    def head_shape(nh, dt=BF):
        return jax.ShapeDtypeStruct((batch, nh, seq, HEAD_DIM), dt)

    vec_spec = pl.BlockSpec((1, HEAD_DIM), lambda i, j: (0, 0))
    tab_spec = pl.BlockSpec((tm, HEAD_DIM), lambda i, j: (i % tpb, 0))
    rope_in = list(rope_tabs) if rope else []
    rope_specs = [tab_spec, tab_spec] if rope else []

    q = _proj(h, ws["q"], ATT_WIDTH, functools.partial(_q_epilogue, rope),
              [q_norm] + rope_in, [vec_spec] + rope_specs,
              head_shape(ATT_HEADS), head_spec(ATT_HEADS), tm, "proj_q_" + tag)

    kv_shapes = [head_shape(ATT_KV_HEADS), head_shape(ATT_KV_HEADS)]
    kv_specs = [head_spec(ATT_KV_HEADS), head_spec(ATT_KV_HEADS)]
    if not rope:
        tok_spec = pl.BlockSpec((tm, KV_WIDTH), lambda i, j: (i, 0))
        kv_shapes += [jax.ShapeDtypeStruct((m, KV_WIDTH), F32)] * 2
        kv_specs += [tok_spec, tok_spec]
    kv = _proj(h, ws["kv"], 2 * KV_WIDTH, functools.partial(_kv_epilogue, rope),
               [k_norm] + rope_in, [vec_spec] + rope_specs,
               kv_shapes, kv_specs, tm, "proj_kv_" + tag)

    def heads(seg, act):
        return _proj(h, ws[seg], HG_WIDTH, functools.partial(_heads_epilogue, act),
                     [], [], head_shape(HG_HEADS), head_spec(HG_HEADS), tm, "proj_" + seg + "_" + tag)

    qh = heads("qh", "silu")
    vh = heads("vh", "none")
    og = heads("og", "silu")

    def forget(seg, lbp):
        return _proj(h, ws[seg], HG_WIDTH, _forget_epilogue,
                     [lbp], [pl.BlockSpec(lbp.shape, lambda i, j: (0, 0))],
                     [head_shape(HG_HEADS, F32), head_shape(HG_HEADS)],
                     [head_spec(HG_HEADS), head_spec(HG_HEADS)], tm, "proj_" + seg + "_" + tag)

    lf_f, k_f = forget("zf", lb_f)
    lf_b, k_b = forget("zb", lb_b)

    tn = 1024
    gates = _proj(h, ws["gates"], tn, _sigmoid_epilogue, [], [],
                  jax.ShapeDtypeStruct((m, 2 * D_MODEL), BF),
                  pl.BlockSpec((tm, tn), lambda i, j: (i, j)), tm, "proj_gates_" + tag)
    return q, kv, qh, vh, og, lf_f, k_f, lf_b, k_b, gates


def _attn_kernel(q_ref, k_ref, v_ref, o_ref, vt_scr):
    @pl.when(pl.program_id(2) == 0)
    def _():
        vt_scr[0:HEAD_DIM, :] = v_ref[0, 0].T
        vt_scr[HEAD_DIM:, :] = jnp.ones((ATTN_ONES_ROWS, vt_scr.shape[1]), BF)

    tq = q_ref.shape[2]
    keys = k_ref.shape[2]
    n_chunks = keys // ATTN_KEY_CHUNK
    n_pass = GQA_GROUP // ATTN_HEADS_PER_PASS
    qs = [jnp.concatenate([q_ref[0, g] for g in range(ps * ATTN_HEADS_PER_PASS,
                                                       (ps + 1) * ATTN_HEADS_PER_PASS)], axis=0)
          for ps in range(n_pass)]
    items = [(c, ps) for c in range(n_chunks) for ps in range(n_pass)]

    def scores(item):
        c, ps = item
        return _dot_nt(k_ref[0, 0, c * ATTN_KEY_CHUNK:(c + 1) * ATTN_KEY_CHUNK, :], qs[ps])

    state = [None] * n_pass
    pending = [scores(it) for it in items[:ATTN_LOOKAHEAD]]
    for i, (c, ps) in enumerate(items):
        ks = slice(c * ATTN_KEY_CHUNK, (c + 1) * ATTN_KEY_CHUNK)
        s = pending.pop(0)
        if i + ATTN_LOOKAHEAD < len(items):
            pending.append(scores(items[i + ATTN_LOOKAHEAD]))
        smax = jnp.max(s, axis=0, keepdims=True)
        if c == 0:
            state[ps] = (smax, _dot(vt_scr[:, ks], jnp.exp2(s - smax).astype(BF)))
        else:
            m, acc = state[ps]
            m_new = jnp.maximum(m, smax)
            p = jnp.exp2(s - m_new).astype(BF)
            state[ps] = (m_new, acc * jnp.exp2(m - m_new) + _dot(vt_scr[:, ks], p))
    for ps in range(n_pass):
        acc = state[ps][1]
        o = acc[0:HEAD_DIM] / acc[HEAD_DIM:HEAD_DIM + 1]
        for i in range(ATTN_HEADS_PER_PASS):
            g = ps * ATTN_HEADS_PER_PASS + i
            o_ref[0, :, g * HEAD_DIM:(g + 1) * HEAD_DIM] = o[:, i * tq:(i + 1) * tq].T.astype(BF)


def _attention(q, k, v, tq, name):
    batch, _, seq, _ = q.shape
    keys = k.shape[2]
    kv_spec = pl.BlockSpec((1, 1, keys, HEAD_DIM), lambda b, kv, t: (b, kv, 0, 0))
    return pl.pallas_call(
        _attn_kernel,
        grid=(batch, ATT_KV_HEADS, seq // tq),
        in_specs=[pl.BlockSpec((1, GQA_GROUP, tq, HEAD_DIM), lambda b, kv, t: (b, kv, t, 0)),
                  kv_spec, kv_spec],
        out_specs=pl.BlockSpec((1, tq, GQA_GROUP * HEAD_DIM), lambda b, kv, t: (b, t, kv)),
        out_shape=jax.ShapeDtypeStruct((batch, seq, ATT_WIDTH), BF),
        scratch_shapes=[pltpu.VMEM((HEAD_DIM + ATTN_ONES_ROWS, keys), BF)],
        compiler_params=_params("arbitrary", "arbitrary", "arbitrary"),
        name=name,
    )(q, k, v)


def _split3(x):
    hi = x.astype(BF)
    r1 = x - hi.astype(F32)
    mid = r1.astype(BF)
    lo = (r1 - mid.astype(F32)).astype(BF)
    return hi, mid, lo


def _block_reference(b3, hb, rev):
    n8 = b3.shape[0]
    if hb < 8:
        def row(r):
            return jnp.broadcast_to(b3[:, r:r + 1, :], b3.shape)
        if hb == 4:
            return row(4 if rev else 3)
        assert hb == 2
        sub = lax.broadcasted_iota(jnp.int32, b3.shape, 1)
        r0, r1 = (2, 6) if rev else (1, 5)
        return jnp.where(sub < 4, row(r0), row(r1))
    g = hb // 8
    nblk = n8 // (2 * g)
    edge = (b3[:, 0:1, :] if rev else b3[:, 7:8, :]).reshape(nblk, 2 * g, 1, b3.shape[2])
    pick = edge[:, g:g + 1] if rev else edge[:, g - 1:g]
    return jnp.broadcast_to(pick, (nblk, 2 * g, 8, b3.shape[2])).reshape(b3.shape)


def _chunk_cumsum(lf, tri_b):
    hi, mid, lo = _split3(lf)
    return _dot(tri_b, hi) + _dot(tri_b, mid) + _dot(tri_b, lo)


def _rows8(x, r):
    return x[r * 8:(r + 1) * 8]


def _hg_products(q, k, lf, b, vt, rev):
    c = q.shape[0]
    n8 = c // 8
    b3 = b.reshape(n8, 8, HEAD_DIM)
    q32 = q.astype(F32)
    k32 = k.astype(F32)
    sub = lax.broadcasted_iota(jnp.int32, (c, HEAD_DIM), 0)
    qbit = 0 if rev else 1
    products = [(HG_DIAG_LEVEL, list(range(n8)), _dot_nt(q, k))]
    hb, j = 1, 0
    while hb < c:
        if hb < 8:
            is_query = ((sub & hb) == 0) if rev else ((sub & hb) != 0)
            if hb == 1:
                arg = jnp.where(is_query, lf, 0.0)
            else:
                arg = -jnp.abs(b - _block_reference(b3, hb, rev).reshape(c, HEAD_DIM))
            x = (jnp.where(is_query, q32, k32) * jnp.exp2(arg)).astype(BF)
            prod = _dot_nt(x, x)
            q_groups = list(range(n8))
        else:
            q_groups = [r for r in range(n8) if ((r * 8 // hb) & 1) == qbit]
            ref = _block_reference(b3, hb, rev).reshape(c, HEAD_DIM)
            mixed = jnp.concatenate([_rows8(q32 if r in q_groups else k32, r) for r in range(n8)], axis=0)
            arg = jnp.concatenate([_rows8(b, r) - _rows8(ref, r) if r in q_groups
                                   else _rows8(ref, r) - _rows8(b, r) for r in range(n8)], axis=0)
            y = mixed * jnp.exp2(arg)
            lhs = jnp.concatenate([_rows8(y, r) for r in q_groups], axis=0).astype(BF)
            prod = _dot_nt(lhs, y.astype(BF))
        products.append((j, q_groups, prod))
        hb, j = 2 * hb, j + 1
    btot = b[0:1, :] if rev else b[c - 1:c, :]
    qs = (q32 * jnp.exp2(b)).astype(BF)
    ks = (k32 * jnp.exp2(btot - b)).astype(BF)
    return products, qs, jnp.exp2(btot), _dot(vt, ks)


def _hg_finish(parts, vt, st, level):
    products, qs, decay, contribution = parts
    a_rows = [None] * (level.shape[0] // 8)
    for code, q_groups, prod in products:
        for i, r in enumerate(q_groups):
            keep = _rows8(level, r) == code
            a_rows[r] = jnp.where(keep, _rows8(prod, i), 0.0 if a_rows[r] is None else a_rows[r])
    a = jnp.concatenate(a_rows, axis=0)
    o = _dot_nt(jnp.concatenate([a.astype(BF), qs], axis=1),
                jnp.concatenate([vt, st.astype(BF)], axis=1))
    return o, st * decay + contribution


def _pair_levels(c, rev):
    t = lax.broadcasted_iota(jnp.int32, (c, c), 0)
    s = lax.broadcasted_iota(jnp.int32, (c, c), 1)
    x = t ^ s
    lvl = jnp.zeros((c, c), jnp.int32)
    p = 2
    while p < c:
        lvl = lvl + jnp.where(x >= p, 1, 0)
        p *= 2
    valid = (s > t) if rev else (s < t)
    return jnp.where(valid, lvl, jnp.where(s == t, HG_DIAG_LEVEL, -1))


def _hgrn_kernel(seq, heads, zero_init, q_ref, kf_ref, kb_ref, lf_ref, lb_ref, v_ref, og_ref, *refs):
    if zero_init:
        hn_ref, o_ref, sf_ref, sb_ref, vt_scr, bf_scr, bb_scr, of_scr, ob_scr = refs
    else:
        s0f_ref, s0b_ref, hn_ref, o_ref, vt_scr, bf_scr, bb_scr, of_scr, ob_scr = refs
    c = HG_CHUNK
    n = seq // c
    t = lax.broadcasted_iota(jnp.int32, (c, c), 0)
    s = lax.broadcasted_iota(jnp.int32, (c, c), 1)
    tril_b = jnp.where(s <= t, 1.0, 0.0).astype(BF)
    triu_b = jnp.where(s >= t, 1.0, 0.0).astype(BF)
    level_f = _pair_levels(c, False)
    level_b = _pair_levels(c, True)

    def rows(i):
        return pl.ds(pl.multiple_of(i * c, c), c)

    def head_body(h, _):
        group = min(n, HG_GROUP)

        def prepare(g, _):
            done = []
            for u in range(group):
                i = g * group + u
                r = rows(i)
                done.append((i, r, v_ref[0, h, r, :].T, _chunk_cumsum(lf_ref[0, h, r, :], tril_b),
                             _chunk_cumsum(lb_ref[0, h, r, :], triu_b)))
            for i, r, vt, bf, bb in done:
                vt_scr[i] = vt
                bf_scr[r, :] = bf
                bb_scr[r, :] = bb
            return 0

        lax.fori_loop(0, n // group, prepare, 0)

        def body(g, carry):
            st_f, st_b = carry
            work = []
            for u in range(group):
                i = g * group + u
                jf, jb = i, n - 1 - i
                rf, rb = rows(jf), rows(jb)
                work.append((rf, jf, _hg_products(q_ref[0, h, rf, :], kf_ref[0, h, rf, :],
                                                  lf_ref[0, h, rf, :], bf_scr[rf, :], vt_scr[jf], False),
                             rb, jb, _hg_products(q_ref[0, h, rb, :], kb_ref[0, h, rb, :],
                                                  lb_ref[0, h, rb, :], bb_scr[rb, :], vt_scr[jb], True)))
            for rf, jf, parts_f, rb, jb, parts_b in work:
                o_f, st_f = _hg_finish(parts_f, vt_scr[jf], st_f, level_f)
                of_scr[rf, :] = o_f
                o_b, st_b = _hg_finish(parts_b, vt_scr[jb], st_b, level_b)
                ob_scr[rb, :] = o_b
            return st_f, st_b

        if zero_init:
            st0 = (jnp.zeros((HEAD_DIM, HEAD_DIM), F32), jnp.zeros((HEAD_DIM, HEAD_DIM), F32))
        else:
            st0 = (s0f_ref[0, h].T, s0b_ref[0, h].T)
        st_f, st_b = lax.fori_loop(0, n // group, body, st0, unroll=min(n // group, HG_UNROLL))
        o = of_scr[...] + ob_scr[...]
        y = _rms(o) * hn_ref[...] * og_ref[0, h].astype(F32)
        o_ref[0, h] = y.astype(BF)
        if zero_init:
            sf_ref[0, h] = st_f.T
            sb_ref[0, h] = st_b.T
        return 0

    lax.fori_loop(0, heads, head_body, 0)


def _hgrn(qh, k_f, k_b, lf_f, lf_b, vh, og, hg_norm, s0f, s0b, heads_per_step, name):
    batch, heads, seq, hd = qh.shape
    hps = heads_per_step
    zero_init = s0f is None
    head_spec = pl.BlockSpec((1, hps, seq, hd), lambda b, h: (b, h, 0, 0))
    state_spec = pl.BlockSpec((1, hps, hd, hd), lambda b, h: (b, h, 0, 0))
    vec_spec = pl.BlockSpec((1, hd), lambda b, h: (0, 0))
    o_shape = jax.ShapeDtypeStruct((batch, heads, seq, hd), BF)
    ins = [qh, k_f, k_b, lf_f, lf_b, vh, og]
    in_specs = [head_spec] * 7
    if zero_init:
        state_shape = jax.ShapeDtypeStruct((batch, heads, hd, hd), F32)
        out_shape = [o_shape, state_shape, state_shape]
        out_specs = [head_spec, state_spec, state_spec]
    else:
        ins += [s0f, s0b]
        in_specs += [state_spec, state_spec]
        out_shape = o_shape
        out_specs = head_spec
    ins.append(hg_norm)
    in_specs.append(vec_spec)
    return pl.pallas_call(
        functools.partial(_hgrn_kernel, seq, hps, zero_init),
        grid=(batch, heads // hps),
        in_specs=in_specs,
        out_specs=out_specs,
        out_shape=out_shape,
        scratch_shapes=[pltpu.VMEM((seq // HG_CHUNK, hd, HG_CHUNK), BF)]
        + [pltpu.VMEM((seq, hd), F32)] * 4,
        compiler_params=_params("arbitrary", "arbitrary"),
        name=name,
    )(*ins)


def _merge_kernel(att_ref, hg_ref, ga_ref, gb_ref, x_ref, wa_ref, wh_ref, wo_ref,
                  g1_ref, sc2_ref, sh2_ref, npost_ref, npre_ref, x1_ref, h2_ref):
    a = _dot(att_ref[...], wa_ref[...])
    hg = jnp.concatenate([hg_ref[0, h] for h in range(HG_HEADS)], axis=1)
    b = _dot(hg, wh_ref[...])
    m = (ga_ref[...].astype(F32) * a + gb_ref[...].astype(F32) * b).astype(BF)
    mo = _dot(m, wo_ref[...])
    x1 = x_ref[...] + g1_ref[0] * (_rms(mo) * npost_ref[...])
    x1_ref[...] = x1
    h2 = _rms(x1) * npre_ref[...]
    h2_ref[...] = (h2 * (1.0 + sc2_ref[0]) + sh2_ref[0]).astype(BF)


def _merge(att, hg, gates, x, wa, wh, wo, g1, sc2, sh2, npost, npre, seq, name):
    m, d = x.shape
    tm = 256
    tpb = seq // tm
    w_att = att.shape[1]
    hg_tpb = hg.shape[2] // tm
    hg_spec = pl.BlockSpec((1, HG_HEADS, tm, HEAD_DIM), lambda i: (i // hg_tpb, 0, i % hg_tpb, 0))

    def const(shape):
        return pl.BlockSpec(shape, lambda i: (0, 0), pipeline_mode=pl.Buffered(1))

    mod_spec = pl.BlockSpec((1, 1, d), lambda i: (i // tpb, 0, 0))
    vec_spec = pl.BlockSpec((1, d), lambda i: (0, 0))
    return pl.pallas_call(
        _merge_kernel,
        grid=(m // tm,),
        in_specs=[
            pl.BlockSpec((tm, w_att), lambda i: (i, 0)),
            hg_spec,
            pl.BlockSpec((tm, d), lambda i: (i, 0)),
            pl.BlockSpec((tm, d), lambda i: (i, 1)),
            pl.BlockSpec((tm, d), lambda i: (i, 0)),
            const(wa.shape), const(wh.shape), const(wo.shape),
            mod_spec, mod_spec, mod_spec, vec_spec, vec_spec,
        ],
        out_specs=[pl.BlockSpec((tm, d), lambda i: (i, 0)), pl.BlockSpec((tm, d), lambda i: (i, 0))],
        out_shape=[jax.ShapeDtypeStruct((m, d), F32), jax.ShapeDtypeStruct((m, d), BF)],
        compiler_params=_params("arbitrary"),
        name=name,
    )(att, hg, gates, gates, x, wa, wh, wo, g1, sc2, sh2, npost, npre)


def _ffn_up_kernel(h_ref, wg_ref, wu_ref, a_ref):
    h = h_ref[...]
    g = _dot(h, wg_ref[...])
    u = _dot(h, wu_ref[...])
    a_ref[...] = (g * jax.nn.sigmoid(g) * u).astype(BF)


def _ffn_down_kernel(a_ref, wo_ref, x1_ref, g2_ref, npost_ref, o_ref):
    y = _rms(_dot(a_ref[...], wo_ref[...])) * npost_ref[...]
    o_ref[...] = x1_ref[...] + g2_ref[0] * y


def _ffn(h2, x1, w_in, w_out, g2, npost, seq, name):
    m, d = x1.shape
    tm, tf = PROJ_TOKENS, 512
    nf = D_FF // tf
    act = pl.pallas_call(
        _ffn_up_kernel,
        grid=(m // tm, nf),
        in_specs=[
            pl.BlockSpec((tm, d), lambda i, f: (i, 0)),
            pl.BlockSpec((d, tf), lambda i, f: (0, f)),
            pl.BlockSpec((d, tf), lambda i, f: (0, nf + f)),
        ],
        out_specs=pl.BlockSpec((tm, tf), lambda i, f: (i, f)),
        out_shape=jax.ShapeDtypeStruct((m, D_FF), BF),
        compiler_params=_params("arbitrary", "arbitrary"),
        name=name + "_up",
    )(h2, w_in, w_in)
    td = 256
    tpb = seq // td
    return pl.pallas_call(
        _ffn_down_kernel,
        grid=(m // td,),
        in_specs=[
            pl.BlockSpec((td, D_FF), lambda i: (i, 0)),
            pl.BlockSpec((D_FF, d), lambda i: (0, 0), pipeline_mode=pl.Buffered(1)),
            pl.BlockSpec((td, d), lambda i: (i, 0)),
            pl.BlockSpec((1, 1, d), lambda i: (i // tpb, 0, 0)),
            pl.BlockSpec((1, d), lambda i: (0, 0)),
        ],
        out_specs=pl.BlockSpec((td, d), lambda i: (i, 0)),
        out_shape=jax.ShapeDtypeStruct((m, d), F32),
        compiler_params=_params("arbitrary"),
        name=name + "_down",
    )(act, w_out, x1, g2, npost)


def _rope_tables(n_tokens):
    rows = n_tokens // GRID_W
    half = HEAD_DIM // 4
    r = jnp.repeat(jnp.arange(rows), GRID_W).astype(F32)
    col = jnp.tile(jnp.arange(GRID_W), rows).astype(F32)
    inv = ROPE_THETA ** (-jnp.arange(half, dtype=F32) / half)
    ar = r[:, None] * inv
    ac = col[:, None] * inv
    cos = jnp.concatenate([jnp.cos(ar), jnp.cos(ar), jnp.cos(ac), jnp.cos(ac)], axis=-1)
    sin = jnp.concatenate([-jnp.sin(ar), jnp.sin(ar), -jnp.sin(ac), jnp.sin(ac)], axis=-1)
    return cos, sin


def _layer(x3, mods, weights, rope_tabs, cache, states, tag):
    batch, seq, d = x3.shape
    m = batch * seq
    x = x3.reshape(m, d)
    (w_in, q_norm, k_norm, lb_f, lb_b, hg_norm, wa, wh, wo, w_ffn_in, w_ffn_out,
     n_pre_mix, n_post_mix, n_pre_ffn, n_post_ffn) = weights
    nb = mods.shape[0]
    sh1, sc1, g1, sh2, sc2, g2 = [mods[:, i].reshape(nb, 1, d) for i in range(6)]
    seq_mod = seq if nb == batch else m

    h = _modnorm(x, n_pre_mix, sc1, sh1, seq_mod)
    q, kv, qh, vh, og, lf_f, k_f, lf_b, k_b, gates = _in_projection(
        h, w_in, batch, seq, rope_tabs, q_norm, k_norm, lb_f, lb_b, tag)

    if cache is None:
        k_bf, v_bf, k_f32, v_f32 = kv
        att = _attention(q, k_bf, v_bf, seq, "attn_" + tag)
    else:
        k_bf, v_bf = kv
        att = _attention(q, jnp.concatenate([k_bf, cache[0]], axis=2),
                         jnp.concatenate([v_bf, cache[1]], axis=2), 256, "attn_" + tag)
        k_f32 = v_f32 = None

    hps = HG_HEADS if seq <= 2 * HG_CHUNK else 1
    if states is None:
        hg, s_f, s_b = _hgrn(qh, k_f, k_b, lf_f, lf_b, vh, og, hg_norm, None, None, hps, "hgrn_" + tag)
    else:
        hg = _hgrn(qh, k_f, k_b, lf_f, lf_b, vh, og, hg_norm, states[0], states[1], hps, "hgrn_" + tag)
        s_f = s_b = None

    x1, h2 = _merge(att.reshape(m, ATT_WIDTH), hg, gates, x, wa, wh, wo,
                    g1, sc2, sh2, n_post_mix, n_pre_ffn, seq_mod, "merge_" + tag)
    y = _ffn(h2, x1, w_ffn_in, w_ffn_out, g2, n_post_ffn, seq_mod, "ffn_" + tag)
    return y.reshape(batch, seq, d), k_f32, v_f32, s_f, s_b


def kernel(x_prompt, x_sample, cache_k, cache_v, state_fwd, state_bwd, c, c_ctx, w_ada, b_ada,
           norm_pre_mix, norm_post_mix, norm_pre_ffn, norm_post_ffn, w_in, q_norm, k_norm,
           lb_fwd, lb_bwd, hg_norm, w_br_att, w_br_hg, w_out, w_ffn_in, w_ffn_out):
    depth = w_in.shape[0]
    assert depth == 1 and lb_fwd.shape[0] == 2
    batch, seq, d = x_prompt.shape
    dec_batch, dec_seq, _ = x_sample.shape
    past = cache_k.shape[2]

    cond = jnp.concatenate([c_ctx[None, :], c, jnp.zeros((8 - 1 - dec_batch, d), F32)], axis=0)
    mods = _ada(cond, w_ada[0], b_ada[0][None, :]).reshape(8, 6, d)

    segments = (("q", OFF_Q, OFF_KV), ("kv", OFF_KV, OFF_QH), ("qh", OFF_QH, OFF_ZF),
                ("zf", OFF_ZF, OFF_ZB), ("zb", OFF_ZB, OFF_VH), ("vh", OFF_VH, OFF_OG),
                ("og", OFF_OG, OFF_GA), ("gates", OFF_GA, OFF_GB + D_MODEL))
    w_in_segments = {name: w_in[0, :, a:b].astype(BF) for name, a, b in segments}
    weights = (
        w_in_segments, q_norm, k_norm, lb_fwd, lb_bwd, hg_norm,
        w_br_att[0].astype(BF), w_br_hg[0].astype(BF), w_out[0].astype(BF),
        w_ffn_in[0].astype(BF), w_ffn_out[0].astype(BF),
        norm_pre_mix, norm_post_mix, norm_pre_ffn, norm_post_ffn,
    )

    y_p, k_c, v_c, s_f, s_b = _layer(x_prompt, mods[0:1], weights, None, None, None, "ctx")

    cache = (
        cache_k[:, 0].transpose(0, 2, 1, 3).astype(BF),
        cache_v[:, 0].transpose(0, 2, 1, 3).astype(BF),
    )
    states = (state_fwd[:, 0], state_bwd[:, 0])
    y_s, _, _, _, _ = _layer(x_sample, mods[1:1 + dec_batch], weights, _rope_tables(dec_seq),
                             cache, states, "lat")

    new_k = k_c.reshape(batch, 1, seq, ATT_KV_HEADS, HEAD_DIM)
    new_v = v_c.reshape(batch, 1, seq, ATT_KV_HEADS, HEAD_DIM)
    return (y_p, y_s, new_k, new_v, s_f[:, None], s_b[:, None])
```

```python
import functools

import jax
import jax.numpy as jnp
from jax import lax
from jax.experimental import pallas as pl
from jax.experimental.pallas import tpu as pltpu

D_MODEL = 2048
HEAD_DIM = 128
ATT_HEADS = 8
ATT_KV_HEADS = 2
GQA_GROUP = ATT_HEADS // ATT_KV_HEADS
ATT_WIDTH = ATT_HEADS * HEAD_DIM
KV_WIDTH = ATT_KV_HEADS * HEAD_DIM
HG_HEADS = 8
HG_WIDTH = HG_HEADS * HEAD_DIM
D_FF = 5632
GRID_W = 64
ROPE_THETA = 10000.0
NORM_EPS = 1e-6
HG_CHUNK = 128
HG_DIAG_LEVEL = 31
HG_GROUP = 8
HG_UNROLL = 1
ATTN_KEY_CHUNK = 256
ATTN_HEADS_PER_PASS = 4
ATTN_ONES_ROWS = 16
ATTN_LOOKAHEAD = 2
LOG2E = 1.4426950408889634

OFF_Q = 0
OFF_KV = ATT_WIDTH
OFF_QH = OFF_KV + 2 * KV_WIDTH
OFF_ZF = OFF_QH + HG_WIDTH
OFF_ZB = OFF_ZF + HG_WIDTH
OFF_VH = OFF_ZB + HG_WIDTH
OFF_OG = OFF_VH + HG_WIDTH
OFF_GA = OFF_OG + HG_WIDTH
OFF_GB = OFF_GA + D_MODEL

BF = jnp.bfloat16
F32 = jnp.float32

VMEM_LIMIT_BYTES = 56 * 1024 * 1024


def _params(*sem):
    return pltpu.CompilerParams(dimension_semantics=sem, vmem_limit_bytes=VMEM_LIMIT_BYTES)


def _rms(x):
    return x * lax.rsqrt(jnp.mean(x * x, axis=-1, keepdims=True) + NORM_EPS)


def _dot(a, b):
    return jnp.dot(a, b, preferred_element_type=F32)


def _dot_nt(a, b):
    return lax.dot_general(a, b, (((1,), (1,)), ((), ())), preferred_element_type=F32)


def _dot_tn(a, b):
    return lax.dot_general(a, b, (((0,), (0,)), ((), ())), preferred_element_type=F32)


def _ada_kernel(c_ref, w_ref, b_ref, o_ref):
    c = c_ref[...]
    s = (c * jax.nn.sigmoid(c)).astype(BF)
    o_ref[...] = _dot(s, w_ref[...].astype(BF)) + b_ref[...]


def _ada(cond, w, b):
    rows, d = cond.shape
    n = w.shape[1]
    tn = 1536
    return pl.pallas_call(
        _ada_kernel,
        grid=(n // tn,),
        in_specs=[
            pl.BlockSpec((rows, d), lambda j: (0, 0)),
            pl.BlockSpec((d, tn), lambda j: (0, j)),
            pl.BlockSpec((1, tn), lambda j: (0, j)),
        ],
        out_specs=pl.BlockSpec((rows, tn), lambda j: (0, j)),
        out_shape=jax.ShapeDtypeStruct((rows, n), F32),
        compiler_params=_params("arbitrary"),
        name="ada_mod",
    )(cond, w, b)


def _modnorm_kernel(x_ref, g_ref, sc_ref, sh_ref, o_ref):
    y = _rms(x_ref[...]) * g_ref[...]
    o_ref[...] = (y * (1.0 + sc_ref[0]) + sh_ref[0]).astype(BF)


def _modnorm(x, g, sc, sh, seq):
    m, d = x.shape
    tm = PROJ_TOKENS
    tpb = seq // tm
    return pl.pallas_call(
        _modnorm_kernel,
        grid=(m // tm,),
        in_specs=[
            pl.BlockSpec((tm, d), lambda i: (i, 0)),
            pl.BlockSpec((1, d), lambda i: (0, 0)),
            pl.BlockSpec((1, 1, d), lambda i: (i // tpb, 0, 0)),
            pl.BlockSpec((1, 1, d), lambda i: (i // tpb, 0, 0)),
        ],
        out_specs=pl.BlockSpec((tm, d), lambda i: (i, 0)),
        out_shape=jax.ShapeDtypeStruct((m, d), BF),
        compiler_params=_params("arbitrary"),
        name="modnorm",
    )(x, g, sc, sh)


def _head_rms(x):
    ones = jnp.ones((HEAD_DIM, HEAD_DIM), BF)
    ss = _dot((x * x).astype(BF), ones)
    return x * lax.rsqrt(ss * (1.0 / HEAD_DIM) + NORM_EPS)


def _rope(y, cos_ref, sin_ref):
    i = lax.broadcasted_iota(jnp.int32, (HEAD_DIM, HEAD_DIM), 0)
    j = lax.broadcasted_iota(jnp.int32, (HEAD_DIM, HEAD_DIM), 1)
    partner = jnp.where(i == (j ^ 32), 1.0, 0.0).astype(BF)
    return y * cos_ref[...] + _dot(y.astype(BF), partner) * sin_ref[...]


def _store_heads(o_ref, h, y):
    nbb, _, ts, _ = o_ref.shape
    for bl in range(nbb):
        o_ref[bl, h] = y[bl * ts:(bl + 1) * ts].astype(o_ref.dtype)


def _q_epilogue(rope, acc, col0, refs):
    if rope:
        qn_ref, cos_ref, sin_ref, o_ref = refs
    else:
        qn_ref, o_ref = refs
    scale = LOG2E * HEAD_DIM ** -0.5
    for hl in range(acc.shape[1] // HEAD_DIM):
        y = _head_rms(acc[:, hl * HEAD_DIM:(hl + 1) * HEAD_DIM]) * qn_ref[...]
        if rope:
            y = _rope(y, cos_ref, sin_ref)
        _store_heads(o_ref, col0 // HEAD_DIM + hl, y * scale)


def _kv_epilogue(rope, acc, col0, refs):
    if rope:
        kn_ref, cos_ref, sin_ref, kb_ref, vb_ref = refs
    else:
        kn_ref, kb_ref, vb_ref, kf_ref, vf_ref = refs
    is_k = col0 < KV_WIDTH
    for hl in range(acc.shape[1] // HEAD_DIM):
        h = (col0 % KV_WIDTH) // HEAD_DIM + hl
        sl = slice(h * HEAD_DIM, (h + 1) * HEAD_DIM)
        y = acc[:, hl * HEAD_DIM:(hl + 1) * HEAD_DIM]
        if is_k:
            y = _head_rms(y) * kn_ref[...]
        if not rope:
            (kf_ref if is_k else vf_ref)[:, sl] = y
        elif is_k:
            y = _rope(y, cos_ref, sin_ref)
        _store_heads(kb_ref if is_k else vb_ref, h, y)


def _heads_epilogue(act, acc, col0, refs):
    (o_ref,) = refs
    for hl in range(acc.shape[1] // HEAD_DIM):
        y = acc[:, hl * HEAD_DIM:(hl + 1) * HEAD_DIM]
        if act == "silu":
            y = y * jax.nn.sigmoid(y)
        _store_heads(o_ref, col0 // HEAD_DIM + hl, y)


def _forget_epilogue(acc, col0, refs):
    lbp_ref, lf_ref, k_ref = refs
    p = lbp_ref[:, col0:col0 + acc.shape[1]]
    e = jnp.exp(p - jnp.max(p, axis=0, keepdims=True))
    lb = e[0:1] / jnp.sum(e, axis=0, keepdims=True)
    z = acc
    t = jnp.exp(-jnp.abs(z))
    r = 1.0 / (1.0 + t)
    tr = t * r
    pos = z >= 0
    sig_p = jnp.where(pos, r, tr)
    sig_n = jnp.where(pos, tr, r)
    logf = jnp.log2(lb + (1.0 - lb) * sig_p)
    kk = (1.0 - lb) * sig_n
    for hl in range(acc.shape[1] // HEAD_DIM):
        sl = slice(hl * HEAD_DIM, (hl + 1) * HEAD_DIM)
        _store_heads(lf_ref, col0 // HEAD_DIM + hl, logf[:, sl])
        _store_heads(k_ref, col0 // HEAD_DIM + hl, kk[:, sl])


def _sigmoid_epilogue(acc, col0, refs):
    (o_ref,) = refs
    o_ref[:, col0:col0 + acc.shape[1]] = jax.nn.sigmoid(acc).astype(BF)


def _proj_kernel(epilogue, h_ref, w_ref, *refs):
    h = h_ref[...]
    starts = list(range(0, w_ref.shape[1], PROJ_COL_GROUP))

    def product(c0):
        return _dot(h, w_ref[:, c0:c0 + PROJ_COL_GROUP])

    nxt = product(starts[0])
    for i, c0 in enumerate(starts):
        acc = nxt
        if i + 1 < len(starts):
            nxt = product(starts[i + 1])
        epilogue(acc, c0, refs)


def _proj(h, w, tn, epilogue, extra, extra_specs, out_shapes, out_specs, tm, name):
    m, d = h.shape
    ncols = w.shape[1]
    return pl.pallas_call(
        functools.partial(_proj_kernel, epilogue),
        grid=(m // tm, ncols // tn),
        in_specs=[
            pl.BlockSpec((tm, d), lambda i, j: (i, 0)),
            pl.BlockSpec((d, tn), lambda i, j: (0, j)),
        ] + extra_specs,
        out_specs=out_specs,
        out_shape=out_shapes,
        compiler_params=_params("arbitrary", "arbitrary"),
        name=name,
    )(h, w, *extra)


PROJ_TOKENS = 1024
PROJ_COL_GROUP = 256


def _in_projection(h, ws, batch, seq, rope_tabs, q_norm, k_norm, lb_f, lb_b, tag):
    m = h.shape[0]
    tm = PROJ_TOKENS
    rope = rope_tabs is not None
    if seq >= tm:
        tpb = seq // tm
        head_block = (1, tm)

        def head_index(i, j):
            return (i // tpb, 0, i % tpb, 0)
    else:
        tpb = 1
        head_block = (tm // seq, seq)

        def head_index(i, j):
            return (i, 0, 0, 0)

    def head_spec(nh):
        return pl.BlockSpec((head_block[0], nh, head_block[1], HEAD_DIM), head_index)

    def head_shape(nh, dt=BF):
        return jax.ShapeDtypeStruct((batch, nh, seq, HEAD_DIM), dt)

    vec_spec = pl.BlockSpec((1, HEAD_DIM), lambda i, j: (0, 0))
    tab_spec = pl.BlockSpec((tm, HEAD_DIM), lambda i, j: (i % tpb, 0))
    rope_in = list(rope_tabs) if rope else []
    rope_specs = [tab_spec, tab_spec] if rope else []

    q = _proj(h, ws["q"], ATT_WIDTH, functools.partial(_q_epilogue, rope),
              [q_norm] + rope_in, [vec_spec] + rope_specs,
              head_shape(ATT_HEADS), head_spec(ATT_HEADS), tm, "proj_q_" + tag)

    kv_shapes = [head_shape(ATT_KV_HEADS), head_shape(ATT_KV_HEADS)]
    kv_specs = [head_spec(ATT_KV_HEADS), head_spec(ATT_KV_HEADS)]
    if not rope:
        tok_spec = pl.BlockSpec((tm, KV_WIDTH), lambda i, j: (i, 0))
        kv_shapes += [jax.ShapeDtypeStruct((m, KV_WIDTH), F32)] * 2
        kv_specs += [tok_spec, tok_spec]
    kv = _proj(h, ws["kv"], 2 * KV_WIDTH, functools.partial(_kv_epilogue, rope),
               [k_norm] + rope_in, [vec_spec] + rope_specs,
               kv_shapes, kv_specs, tm, "proj_kv_" + tag)

    def heads(seg, act):
        return _proj(h, ws[seg], HG_WIDTH, functools.partial(_heads_epilogue, act),
                     [], [], head_shape(HG_HEADS), head_spec(HG_HEADS), tm, "proj_" + seg + "_" + tag)

    qh = heads("qh", "silu")
    vh = heads("vh", "none")
    og = heads("og", "silu")

    def forget(seg, lbp):
        return _proj(h, ws[seg], HG_WIDTH, _forget_epilogue,
                     [lbp], [pl.BlockSpec(lbp.shape, lambda i, j: (0, 0))],
                     [head_shape(HG_HEADS, F32), head_shape(HG_HEADS)],
                     [head_spec(HG_HEADS), head_spec(HG_HEADS)], tm, "proj_" + seg + "_" + tag)

    lf_f, k_f = forget("zf", lb_f)
    lf_b, k_b = forget("zb", lb_b)

    tn = 1024
    gates = _proj(h, ws["gates"], tn, _sigmoid_epilogue, [], [],
                  jax.ShapeDtypeStruct((m, 2 * D_MODEL), BF),
                  pl.BlockSpec((tm, tn), lambda i, j: (i, j)), tm, "proj_gates_" + tag)
    return q, kv, qh, vh, og, lf_f, k_f, lf_b, k_b, gates


def _attn_kernel(q_ref, k_ref, v_ref, o_ref, vt_scr):
    @pl.when(pl.program_id(2) == 0)
    def _():
        vt_scr[0:HEAD_DIM, :] = v_ref[0, 0].T
        vt_scr[HEAD_DIM:, :] = jnp.ones((ATTN_ONES_ROWS, vt_scr.shape[1]), BF)

    tq = q_ref.shape[2]
    keys = k_ref.shape[2]
    n_chunks = keys // ATTN_KEY_CHUNK
    n_pass = GQA_GROUP // ATTN_HEADS_PER_PASS
    qs = [jnp.concatenate([q_ref[0, g] for g in range(ps * ATTN_HEADS_PER_PASS,
                                                       (ps + 1) * ATTN_HEADS_PER_PASS)], axis=0)
          for ps in range(n_pass)]
    items = [(c, ps) for c in range(n_chunks) for ps in range(n_pass)]

    def scores(item):
        c, ps = item
        return _dot_nt(k_ref[0, 0, c * ATTN_KEY_CHUNK:(c + 1) * ATTN_KEY_CHUNK, :], qs[ps])

    state = [None] * n_pass
    pending = [scores(it) for it in items[:ATTN_LOOKAHEAD]]
    for i, (c, ps) in enumerate(items):
        ks = slice(c * ATTN_KEY_CHUNK, (c + 1) * ATTN_KEY_CHUNK)
        s = pending.pop(0)
        if i + ATTN_LOOKAHEAD < len(items):
            pending.append(scores(items[i + ATTN_LOOKAHEAD]))
        smax = jnp.max(s, axis=0, keepdims=True)
        if c == 0:
            state[ps] = (smax, _dot(vt_scr[:, ks], jnp.exp2(s - smax).astype(BF)))
        else:
            m, acc = state[ps]
            m_new = jnp.maximum(m, smax)
            p = jnp.exp2(s - m_new).astype(BF)
            state[ps] = (m_new, acc * jnp.exp2(m - m_new) + _dot(vt_scr[:, ks], p))
    for ps in range(n_pass):
        acc = state[ps][1]
        o = acc[0:HEAD_DIM] / acc[HEAD_DIM:HEAD_DIM + 1]
        for i in range(ATTN_HEADS_PER_PASS):
            g = ps * ATTN_HEADS_PER_PASS + i
            o_ref[0, :, g * HEAD_DIM:(g + 1) * HEAD_DIM] = o[:, i * tq:(i + 1) * tq].T.astype(BF)


def _attention(q, k, v, tq, name):
    batch, _, seq, _ = q.shape
    keys = k.shape[2]
    kv_spec = pl.BlockSpec((1, 1, keys, HEAD_DIM), lambda b, kv, t: (b, kv, 0, 0))
    return pl.pallas_call(
        _attn_kernel,
        grid=(batch, ATT_KV_HEADS, seq // tq),
        in_specs=[pl.BlockSpec((1, GQA_GROUP, tq, HEAD_DIM), lambda b, kv, t: (b, kv, t, 0)),
                  kv_spec, kv_spec],
        out_specs=pl.BlockSpec((1, tq, GQA_GROUP * HEAD_DIM), lambda b, kv, t: (b, t, kv)),
        out_shape=jax.ShapeDtypeStruct((batch, seq, ATT_WIDTH), BF),
        scratch_shapes=[pltpu.VMEM((HEAD_DIM + ATTN_ONES_ROWS, keys), BF)],
        compiler_params=_params("arbitrary", "arbitrary", "arbitrary"),
        name=name,
    )(q, k, v)


def _split3(x):
    hi = x.astype(BF)
    r1 = x - hi.astype(F32)
    mid = r1.astype(BF)
    lo = (r1 - mid.astype(F32)).astype(BF)
    return hi, mid, lo


def _block_reference(b3, hb, rev):
    n8 = b3.shape[0]
    if hb < 8:
        def row(r):
            return jnp.broadcast_to(b3[:, r:r + 1, :], b3.shape)
        if hb == 4:
            return row(4 if rev else 3)
        assert hb == 2
        sub = lax.broadcasted_iota(jnp.int32, b3.shape, 1)
        r0, r1 = (2, 6) if rev else (1, 5)
        return jnp.where(sub < 4, row(r0), row(r1))
    g = hb // 8
    nblk = n8 // (2 * g)
    edge = (b3[:, 0:1, :] if rev else b3[:, 7:8, :]).reshape(nblk, 2 * g, 1, b3.shape[2])
    pick = edge[:, g:g + 1] if rev else edge[:, g - 1:g]
    return jnp.broadcast_to(pick, (nblk, 2 * g, 8, b3.shape[2])).reshape(b3.shape)


def _chunk_cumsum(lf, tri_b):
    hi, mid, lo = _split3(lf)
    return _dot(tri_b, hi) + _dot(tri_b, mid) + _dot(tri_b, lo)


def _rows8(x, r):
    return x[r * 8:(r + 1) * 8]


def _hg_products(q, k, lf, b, vt, rev):
    c = q.shape[0]
    n8 = c // 8
    b3 = b.reshape(n8, 8, HEAD_DIM)
    q32 = q.astype(F32)
    k32 = k.astype(F32)
    sub = lax.broadcasted_iota(jnp.int32, (c, HEAD_DIM), 0)
    qbit = 0 if rev else 1
    products = [(HG_DIAG_LEVEL, list(range(n8)), _dot_nt(q, k))]
    hb, j = 1, 0
    while hb < c:
        if hb < 8:
            is_query = ((sub & hb) == 0) if rev else ((sub & hb) != 0)
            if hb == 1:
                arg = jnp.where(is_query, lf, 0.0)
            else:
                arg = -jnp.abs(b - _block_reference(b3, hb, rev).reshape(c, HEAD_DIM))
            x = (jnp.where(is_query, q32, k32) * jnp.exp2(arg)).astype(BF)
            prod = _dot_nt(x, x)
            q_groups = list(range(n8))
        else:
            q_groups = [r for r in range(n8) if ((r * 8 // hb) & 1) == qbit]
            ref = _block_reference(b3, hb, rev).reshape(c, HEAD_DIM)
            mixed = jnp.concatenate([_rows8(q32 if r in q_groups else k32, r) for r in range(n8)], axis=0)
            arg = jnp.concatenate([_rows8(b, r) - _rows8(ref, r) if r in q_groups
                                   else _rows8(ref, r) - _rows8(b, r) for r in range(n8)], axis=0)
            y = mixed * jnp.exp2(arg)
            lhs = jnp.concatenate([_rows8(y, r) for r in q_groups], axis=0).astype(BF)
            prod = _dot_nt(lhs, y.astype(BF))
        products.append((j, q_groups, prod))
        hb, j = 2 * hb, j + 1
    btot = b[0:1, :] if rev else b[c - 1:c, :]
    qs = (q32 * jnp.exp2(b)).astype(BF)
    ks = (k32 * jnp.exp2(btot - b)).astype(BF)
    return products, qs, jnp.exp2(btot), _dot(vt, ks)


def _hg_finish(parts, vt, st, level):
    products, qs, decay, contribution = parts
    a_rows = [None] * (level.shape[0] // 8)
    for code, q_groups, prod in products:
        for i, r in enumerate(q_groups):
            keep = _rows8(level, r) == code
            a_rows[r] = jnp.where(keep, _rows8(prod, i), 0.0 if a_rows[r] is None else a_rows[r])
    a = jnp.concatenate(a_rows, axis=0)
    o = _dot_nt(jnp.concatenate([a.astype(BF), qs], axis=1),
                jnp.concatenate([vt, st.astype(BF)], axis=1))
    return o, st * decay + contribution


def _pair_levels(c, rev):
    t = lax.broadcasted_iota(jnp.int32, (c, c), 0)
    s = lax.broadcasted_iota(jnp.int32, (c, c), 1)
    x = t ^ s
    lvl = jnp.zeros((c, c), jnp.int32)
    p = 2
    while p < c:
        lvl = lvl + jnp.where(x >= p, 1, 0)
        p *= 2
    valid = (s > t) if rev else (s < t)
    return jnp.where(valid, lvl, jnp.where(s == t, HG_DIAG_LEVEL, -1))


def _hgrn_kernel(seq, heads, zero_init, q_ref, kf_ref, kb_ref, lf_ref, lb_ref, v_ref, og_ref, *refs):
    if zero_init:
        hn_ref, o_ref, sf_ref, sb_ref, vt_scr, bf_scr, bb_scr, of_scr, ob_scr = refs
    else:
        s0f_ref, s0b_ref, hn_ref, o_ref, vt_scr, bf_scr, bb_scr, of_scr, ob_scr = refs
    c = HG_CHUNK
    n = seq // c
    t = lax.broadcasted_iota(jnp.int32, (c, c), 0)
    s = lax.broadcasted_iota(jnp.int32, (c, c), 1)
    tril_b = jnp.where(s <= t, 1.0, 0.0).astype(BF)
    triu_b = jnp.where(s >= t, 1.0, 0.0).astype(BF)
    level_f = _pair_levels(c, False)
    level_b = _pair_levels(c, True)

    def rows(i):
        return pl.ds(pl.multiple_of(i * c, c), c)

    def head_body(h, _):
        group = min(n, HG_GROUP)

        def prepare(g, _):
            done = []
            for u in range(group):
                i = g * group + u
                r = rows(i)
                done.append((i, r, v_ref[0, h, r, :].T, _chunk_cumsum(lf_ref[0, h, r, :], tril_b),
                             _chunk_cumsum(lb_ref[0, h, r, :], triu_b)))
            for i, r, vt, bf, bb in done:
                vt_scr[i] = vt
                bf_scr[r, :] = bf
                bb_scr[r, :] = bb
            return 0

        lax.fori_loop(0, n // group, prepare, 0)

        def body(g, carry):
            st_f, st_b = carry
            work = []
            for u in range(group):
                i = g * group + u
                jf, jb = i, n - 1 - i
                rf, rb = rows(jf), rows(jb)
                work.append((rf, jf, _hg_products(q_ref[0, h, rf, :], kf_ref[0, h, rf, :],
                                                  lf_ref[0, h, rf, :], bf_scr[rf, :], vt_scr[jf], False),
                             rb, jb, _hg_products(q_ref[0, h, rb, :], kb_ref[0, h, rb, :],
                                                  lb_ref[0, h, rb, :], bb_scr[rb, :], vt_scr[jb], True)))
            for rf, jf, parts_f, rb, jb, parts_b in work:
                o_f, st_f = _hg_finish(parts_f, vt_scr[jf], st_f, level_f)
                of_scr[rf, :] = o_f
                o_b, st_b = _hg_finish(parts_b, vt_scr[jb], st_b, level_b)
                ob_scr[rb, :] = o_b
            return st_f, st_b

        if zero_init:
            st0 = (jnp.zeros((HEAD_DIM, HEAD_DIM), F32), jnp.zeros((HEAD_DIM, HEAD_DIM), F32))
        else:
            st0 = (s0f_ref[0, h].T, s0b_ref[0, h].T)
        st_f, st_b = lax.fori_loop(0, n // group, body, st0, unroll=min(n // group, HG_UNROLL))
        o = of_scr[...] + ob_scr[...]
        y = _rms(o) * hn_ref[...] * og_ref[0, h].astype(F32)
        o_ref[0, h] = y.astype(BF)
        if zero_init:
            sf_ref[0, h] = st_f.T
            sb_ref[0, h] = st_b.T
        return 0

    lax.fori_loop(0, heads, head_body, 0)


def _hgrn(qh, k_f, k_b, lf_f, lf_b, vh, og, hg_norm, s0f, s0b, heads_per_step, name):
    batch, heads, seq, hd = qh.shape
    hps = heads_per_step
    zero_init = s0f is None
    head_spec = pl.BlockSpec((1, hps, seq, hd), lambda b, h: (b, h, 0, 0))
    state_spec = pl.BlockSpec((1, hps, hd, hd), lambda b, h: (b, h, 0, 0))
    vec_spec = pl.BlockSpec((1, hd), lambda b, h: (0, 0))
    o_shape = jax.ShapeDtypeStruct((batch, heads, seq, hd), BF)
    ins = [qh, k_f, k_b, lf_f, lf_b, vh, og]
    in_specs = [head_spec] * 7
    if zero_init:
        state_shape = jax.ShapeDtypeStruct((batch, heads, hd, hd), F32)
        out_shape = [o_shape, state_shape, state_shape]
        out_specs = [head_spec, state_spec, state_spec]
    else:
        ins += [s0f, s0b]
        in_specs += [state_spec, state_spec]
        out_shape = o_shape
        out_specs = head_spec
    ins.append(hg_norm)
    in_specs.append(vec_spec)
    return pl.pallas_call(
        functools.partial(_hgrn_kernel, seq, hps, zero_init),
        grid=(batch, heads // hps),
        in_specs=in_specs,
        out_specs=out_specs,
        out_shape=out_shape,
        scratch_shapes=[pltpu.VMEM((seq // HG_CHUNK, hd, HG_CHUNK), BF)]
        + [pltpu.VMEM((seq, hd), F32)] * 4,
        compiler_params=_params("arbitrary", "arbitrary"),
        name=name,
    )(*ins)


def _merge_kernel(att_ref, hg_ref, ga_ref, gb_ref, x_ref, wa_ref, wh_ref, wo_ref,
                  g1_ref, sc2_ref, sh2_ref, npost_ref, npre_ref, x1_ref, h2_ref):
    a = _dot(att_ref[...], wa_ref[...])
    hg = jnp.concatenate([hg_ref[0, h] for h in range(HG_HEADS)], axis=1)
    b = _dot(hg, wh_ref[...])
    m = (ga_ref[...].astype(F32) * a + gb_ref[...].astype(F32) * b).astype(BF)
    mo = _dot(m, wo_ref[...])
    x1 = x_ref[...] + g1_ref[0] * (_rms(mo) * npost_ref[...])
    x1_ref[...] = x1
    h2 = _rms(x1) * npre_ref[...]
    h2_ref[...] = (h2 * (1.0 + sc2_ref[0]) + sh2_ref[0]).astype(BF)


def _merge(att, hg, gates, x, wa, wh, wo, g1, sc2, sh2, npost, npre, seq, name):
    m, d = x.shape
    tm = 256
    tpb = seq // tm
    w_att = att.shape[1]
    hg_tpb = hg.shape[2] // tm
    hg_spec = pl.BlockSpec((1, HG_HEADS, tm, HEAD_DIM), lambda i: (i // hg_tpb, 0, i % hg_tpb, 0))

    def const(shape):
        return pl.BlockSpec(shape, lambda i: (0, 0), pipeline_mode=pl.Buffered(1))

    mod_spec = pl.BlockSpec((1, 1, d), lambda i: (i // tpb, 0, 0))
    vec_spec = pl.BlockSpec((1, d), lambda i: (0, 0))
    return pl.pallas_call(
        _merge_kernel,
        grid=(m // tm,),
        in_specs=[
            pl.BlockSpec((tm, w_att), lambda i: (i, 0)),
            hg_spec,
            pl.BlockSpec((tm, d), lambda i: (i, 0)),
            pl.BlockSpec((tm, d), lambda i: (i, 1)),
            pl.BlockSpec((tm, d), lambda i: (i, 0)),
            const(wa.shape), const(wh.shape), const(wo.shape),
            mod_spec, mod_spec, mod_spec, vec_spec, vec_spec,
        ],
        out_specs=[pl.BlockSpec((tm, d), lambda i: (i, 0)), pl.BlockSpec((tm, d), lambda i: (i, 0))],
        out_shape=[jax.ShapeDtypeStruct((m, d), F32), jax.ShapeDtypeStruct((m, d), BF)],
        compiler_params=_params("arbitrary"),
        name=name,
    )(att, hg, gates, gates, x, wa, wh, wo, g1, sc2, sh2, npost, npre)


def _ffn_up_kernel(h_ref, wg_ref, wu_ref, a_ref):
    h = h_ref[...]
    g = _dot(h, wg_ref[...])
    u = _dot(h, wu_ref[...])
    a_ref[...] = (g * jax.nn.sigmoid(g) * u).astype(BF)


def _ffn_down_kernel(a_ref, wo_ref, x1_ref, g2_ref, npost_ref, o_ref):
    y = _rms(_dot(a_ref[...], wo_ref[...])) * npost_ref[...]
    o_ref[...] = x1_ref[...] + g2_ref[0] * y


def _ffn(h2, x1, w_in, w_out, g2, npost, seq, name):
    m, d = x1.shape
    tm, tf = PROJ_TOKENS, 512
    nf = D_FF // tf
    act = pl.pallas_call(
        _ffn_up_kernel,
        grid=(m // tm, nf),
        in_specs=[
            pl.BlockSpec((tm, d), lambda i, f: (i, 0)),
            pl.BlockSpec((d, tf), lambda i, f: (0, f)),
            pl.BlockSpec((d, tf), lambda i, f: (0, nf + f)),
        ],
        out_specs=pl.BlockSpec((tm, tf), lambda i, f: (i, f)),
        out_shape=jax.ShapeDtypeStruct((m, D_FF), BF),
        compiler_params=_params("arbitrary", "arbitrary"),
        name=name + "_up",
    )(h2, w_in, w_in)
    td = 256
    tpb = seq // td
    return pl.pallas_call(
        _ffn_down_kernel,
        grid=(m // td,),
        in_specs=[
            pl.BlockSpec((td, D_FF), lambda i: (i, 0)),
            pl.BlockSpec((D_FF, d), lambda i: (0, 0), pipeline_mode=pl.Buffered(1)),
            pl.BlockSpec((td, d), lambda i: (i, 0)),
            pl.BlockSpec((1, 1, d), lambda i: (i // tpb, 0, 0)),
            pl.BlockSpec((1, d), lambda i: (0, 0)),
        ],
        out_specs=pl.BlockSpec((td, d), lambda i: (i, 0)),
        out_shape=jax.ShapeDtypeStruct((m, d), F32),
        compiler_params=_params("arbitrary"),
        name=name + "_down",
    )(act, w_out, x1, g2, npost)


def _rope_tables(n_tokens):
    rows = n_tokens // GRID_W
    half = HEAD_DIM // 4
    r = jnp.repeat(jnp.arange(rows), GRID_W).astype(F32)
    col = jnp.tile(jnp.arange(GRID_W), rows).astype(F32)
    inv = ROPE_THETA ** (-jnp.arange(half, dtype=F32) / half)
    ar = r[:, None] * inv
    ac = col[:, None] * inv
    cos = jnp.concatenate([jnp.cos(ar), jnp.cos(ar), jnp.cos(ac), jnp.cos(ac)], axis=-1)
    sin = jnp.concatenate([-jnp.sin(ar), jnp.sin(ar), -jnp.sin(ac), jnp.sin(ac)], axis=-1)
    return cos, sin


def _layer(x3, mods, weights, rope_tabs, cache, states, tag):
    batch, seq, d = x3.shape
    m = batch * seq
    x = x3.reshape(m, d)
    (w_in, q_norm, k_norm, lb_f, lb_b, hg_norm, wa, wh, wo, w_ffn_in, w_ffn_out,
     n_pre_mix, n_post_mix, n_pre_ffn, n_post_ffn) = weights
    nb = mods.shape[0]
    sh1, sc1, g1, sh2, sc2, g2 = [mods[:, i].reshape(nb, 1, d) for i in range(6)]
    seq_mod = seq if nb == batch else m

    h = _modnorm(x, n_pre_mix, sc1, sh1, seq_mod)
    q, kv, qh, vh, og, lf_f, k_f, lf_b, k_b, gates = _in_projection(
        h, w_in, batch, seq, rope_tabs, q_norm, k_norm, lb_f, lb_b, tag)

    if cache is None:
        k_bf, v_bf, k_f32, v_f32 = kv
        att = _attention(q, k_bf, v_bf, seq, "attn_" + tag)
    else:
        k_bf, v_bf = kv
        att = _attention(q, jnp.concatenate([k_bf, cache[0]], axis=2),
                         jnp.concatenate([v_bf, cache[1]], axis=2), 256, "attn_" + tag)
        k_f32 = v_f32 = None

    hps = HG_HEADS if seq <= 2 * HG_CHUNK else 1
    if states is None:
        hg, s_f, s_b = _hgrn(qh, k_f, k_b, lf_f, lf_b, vh, og, hg_norm, None, None, hps, "hgrn_" + tag)
    else:
        hg = _hgrn(qh, k_f, k_b, lf_f, lf_b, vh, og, hg_norm, states[0], states[1], hps, "hgrn_" + tag)
        s_f = s_b = None

    x1, h2 = _merge(att.reshape(m, ATT_WIDTH), hg, gates, x, wa, wh, wo,
                    g1, sc2, sh2, n_post_mix, n_pre_ffn, seq_mod, "merge_" + tag)
    y = _ffn(h2, x1, w_ffn_in, w_ffn_out, g2, n_post_ffn, seq_mod, "ffn_" + tag)
    return y.reshape(batch, seq, d), k_f32, v_f32, s_f, s_b


def kernel(x_prompt, x_sample, cache_k, cache_v, state_fwd, state_bwd, c, c_ctx, w_ada, b_ada,
           norm_pre_mix, norm_post_mix, norm_pre_ffn, norm_post_ffn, w_in, q_norm, k_norm,
           lb_fwd, lb_bwd, hg_norm, w_br_att, w_br_hg, w_out, w_ffn_in, w_ffn_out):
    depth = w_in.shape[0]
    assert depth == 1 and lb_fwd.shape[0] == 2
    batch, seq, d = x_prompt.shape
    dec_batch, dec_seq, _ = x_sample.shape
    past = cache_k.shape[2]

    cond = jnp.concatenate([c_ctx[None, :], c, jnp.zeros((8 - 1 - dec_batch, d), F32)], axis=0)
    mods = _ada(cond, w_ada[0], b_ada[0][None, :]).reshape(8, 6, d)

    segments = (("q", OFF_Q, OFF_KV), ("kv", OFF_KV, OFF_QH), ("qh", OFF_QH, OFF_ZF),
                ("zf", OFF_ZF, OFF_ZB), ("zb", OFF_ZB, OFF_VH), ("vh", OFF_VH, OFF_OG),
                ("og", OFF_OG, OFF_GA), ("gates", OFF_GA, OFF_GB + D_MODEL))
    w_in_segments = {name: w_in[0, :, a:b].astype(BF) for name, a, b in segments}
    weights = (
        w_in_segments, q_norm, k_norm, lb_fwd, lb_bwd, hg_norm,
        w_br_att[0].astype(BF), w_br_hg[0].astype(BF), w_out[0].astype(BF),
        w_ffn_in[0].astype(BF), w_ffn_out[0].astype(BF),
        norm_pre_mix, norm_post_mix, norm_pre_ffn, norm_post_ffn,
    )

    y_p, k_c, v_c, s_f, s_b = _layer(x_prompt, mods[0:1], weights, None, None, None, "ctx")

    cache = (
        cache_k[:, 0].transpose(0, 2, 1, 3).astype(BF),
        cache_v[:, 0].transpose(0, 2, 1, 3).astype(BF),
    )
    states = (state_fwd[:, 0], state_bwd[:, 0])
    y_s, _, _, _, _ = _layer(x_sample, mods[1:1 + dec_batch], weights, _rope_tables(dec_seq),
                             cache, states, "lat")

    new_k = k_c.reshape(batch, 1, seq, ATT_KV_HEADS, HEAD_DIM)
    new_v = v_c.reshape(batch, 1, seq, ATT_KV_HEADS, HEAD_DIM)
    return (y_p, y_s, new_k, new_v, s_f[:, None], s_b[:, None])
```

```python
import functools

import jax
import jax.numpy as jnp
from jax import lax
from jax.experimental import pallas as pl
from jax.experimental.pallas import tpu as pltpu

D_MODEL = 2048
HEAD_DIM = 128
ATT_HEADS = 8
ATT_KV_HEADS = 2
GQA_GROUP = ATT_HEADS // ATT_KV_HEADS
ATT_WIDTH = ATT_HEADS * HEAD_DIM
KV_WIDTH = ATT_KV_HEADS * HEAD_DIM
HG_HEADS = 8
HG_WIDTH = HG_HEADS * HEAD_DIM
D_FF = 5632
GRID_W = 64
ROPE_THETA = 10000.0
NORM_EPS = 1e-6
HG_CHUNK = 128
HG_DIAG_LEVEL = 31
HG_GROUP = 8
HG_UNROLL = 1
HG_MAX_HEADS_PER_TRIP = 2
ATTN_KEY_CHUNK = 256
ATTN_HEADS_PER_PASS = 4
ATTN_ONES_ROWS = 16
ATTN_LOOKAHEAD = 2
LOG2E = 1.4426950408889634

OFF_Q = 0
OFF_KV = ATT_WIDTH
OFF_QH = OFF_KV + 2 * KV_WIDTH
OFF_ZF = OFF_QH + HG_WIDTH
OFF_ZB = OFF_ZF + HG_WIDTH
OFF_VH = OFF_ZB + HG_WIDTH
OFF_OG = OFF_VH + HG_WIDTH
OFF_GA = OFF_OG + HG_WIDTH
OFF_GB = OFF_GA + D_MODEL

BF = jnp.bfloat16
F32 = jnp.float32

VMEM_LIMIT_BYTES = 56 * 1024 * 1024


def _params(*sem):
    return pltpu.CompilerParams(dimension_semantics=sem, vmem_limit_bytes=VMEM_LIMIT_BYTES)


def _rms(x):
    return x * lax.rsqrt(jnp.mean(x * x, axis=-1, keepdims=True) + NORM_EPS)


def _dot(a, b):
    return jnp.dot(a, b, preferred_element_type=F32)


def _dot_nt(a, b):
    return lax.dot_general(a, b, (((1,), (1,)), ((), ())), preferred_element_type=F32)


def _dot_tn(a, b):
    return lax.dot_general(a, b, (((0,), (0,)), ((), ())), preferred_element_type=F32)


def _ada_kernel(c_ref, w_ref, b_ref, o_ref):
    c = c_ref[...]
    s = (c * jax.nn.sigmoid(c)).astype(BF)
    o_ref[...] = _dot(s, w_ref[...].astype(BF)) + b_ref[...]


def _ada(cond, w, b):
    rows, d = cond.shape
    n = w.shape[1]
    tn = 1536
    return pl.pallas_call(
        _ada_kernel,
        grid=(n // tn,),
        in_specs=[
            pl.BlockSpec((rows, d), lambda j: (0, 0)),
            pl.BlockSpec((d, tn), lambda j: (0, j)),
            pl.BlockSpec((1, tn), lambda j: (0, j)),
        ],
        out_specs=pl.BlockSpec((rows, tn), lambda j: (0, j)),
        out_shape=jax.ShapeDtypeStruct((rows, n), F32),
        compiler_params=_params("arbitrary"),
        name="ada_mod",
    )(cond, w, b)


def _modnorm_kernel(x_ref, g_ref, sc_ref, sh_ref, o_ref):
    y = _rms(x_ref[...]) * g_ref[...]
    o_ref[...] = (y * (1.0 + sc_ref[0]) + sh_ref[0]).astype(BF)


def _modnorm(x, g, sc, sh, seq):
    m, d = x.shape
    tm = PROJ_TOKENS
    tpb = seq // tm
    return pl.pallas_call(
        _modnorm_kernel,
        grid=(m // tm,),
        in_specs=[
            pl.BlockSpec((tm, d), lambda i: (i, 0)),
            pl.BlockSpec((1, d), lambda i: (0, 0)),
            pl.BlockSpec((1, 1, d), lambda i: (i // tpb, 0, 0)),
            pl.BlockSpec((1, 1, d), lambda i: (i // tpb, 0, 0)),
        ],
        out_specs=pl.BlockSpec((tm, d), lambda i: (i, 0)),
        out_shape=jax.ShapeDtypeStruct((m, d), BF),
        compiler_params=_params("arbitrary"),
        name="modnorm",
    )(x, g, sc, sh)


def _head_rms(x):
    ones = jnp.ones((HEAD_DIM, HEAD_DIM), BF)
    ss = _dot((x * x).astype(BF), ones)
    return x * lax.rsqrt(ss * (1.0 / HEAD_DIM) + NORM_EPS)


def _rope(y, cos_ref, sin_ref):
    i = lax.broadcasted_iota(jnp.int32, (HEAD_DIM, HEAD_DIM), 0)
    j = lax.broadcasted_iota(jnp.int32, (HEAD_DIM, HEAD_DIM), 1)
    partner = jnp.where(i == (j ^ 32), 1.0, 0.0).astype(BF)
    return y * cos_ref[...] + _dot(y.astype(BF), partner) * sin_ref[...]


def _store_heads(o_ref, h, y):
    nbb, _, ts, _ = o_ref.shape
    for bl in range(nbb):
        o_ref[bl, h] = y[bl * ts:(bl + 1) * ts].astype(o_ref.dtype)


def _q_epilogue(rope, acc, col0, refs):
    if rope:
        qn_ref, cos_ref, sin_ref, o_ref = refs
    else:
        qn_ref, o_ref = refs
    scale = LOG2E * HEAD_DIM ** -0.5
    for hl in range(acc.shape[1] // HEAD_DIM):
        y = _head_rms(acc[:, hl * HEAD_DIM:(hl + 1) * HEAD_DIM]) * qn_ref[...]
        if rope:
            y = _rope(y, cos_ref, sin_ref)
        _store_heads(o_ref, col0 // HEAD_DIM + hl, y * scale)


def _kv_epilogue(rope, acc, col0, refs):
    if rope:
        kn_ref, cos_ref, sin_ref, kb_ref, vb_ref = refs
    else:
        kn_ref, kb_ref, vb_ref, kf_ref, vf_ref = refs
    is_k = col0 < KV_WIDTH
    for hl in range(acc.shape[1] // HEAD_DIM):
        h = (col0 % KV_WIDTH) // HEAD_DIM + hl
        sl = slice(h * HEAD_DIM, (h + 1) * HEAD_DIM)
        y = acc[:, hl * HEAD_DIM:(hl + 1) * HEAD_DIM]
        if is_k:
            y = _head_rms(y) * kn_ref[...]
        if not rope:
            (kf_ref if is_k else vf_ref)[:, sl] = y
        elif is_k:
            y = _rope(y, cos_ref, sin_ref)
        _store_heads(kb_ref if is_k else vb_ref, h, y)


def _heads_epilogue(act, acc, col0, refs):
    (o_ref,) = refs
    for hl in range(acc.shape[1] // HEAD_DIM):
        y = acc[:, hl * HEAD_DIM:(hl + 1) * HEAD_DIM]
        if act == "silu":
            y = y * jax.nn.sigmoid(y)
        _store_heads(o_ref, col0 // HEAD_DIM + hl, y)


def _forget_epilogue(acc, col0, refs):
    lbp_ref, lf_ref, k_ref = refs
    p = lbp_ref[:, col0:col0 + acc.shape[1]]
    e = jnp.exp(p - jnp.max(p, axis=0, keepdims=True))
    lb = e[0:1] / jnp.sum(e, axis=0, keepdims=True)
    z = acc
    t = jnp.exp(-jnp.abs(z))
    r = 1.0 / (1.0 + t)
    tr = t * r
    pos = z >= 0
    sig_p = jnp.where(pos, r, tr)
    sig_n = jnp.where(pos, tr, r)
    logf = jnp.log2(lb + (1.0 - lb) * sig_p)
    kk = (1.0 - lb) * sig_n
    for hl in range(acc.shape[1] // HEAD_DIM):
        sl = slice(hl * HEAD_DIM, (hl + 1) * HEAD_DIM)
        _store_heads(lf_ref, col0 // HEAD_DIM + hl, logf[:, sl])
        _store_heads(k_ref, col0 // HEAD_DIM + hl, kk[:, sl])


def _sigmoid_epilogue(acc, col0, refs):
    (o_ref,) = refs
    o_ref[:, col0:col0 + acc.shape[1]] = jax.nn.sigmoid(acc).astype(BF)


def _proj_kernel(epilogue, h_ref, w_ref, *refs):
    h = h_ref[...]
    starts = list(range(0, w_ref.shape[1], PROJ_COL_GROUP))

    def product(c0):
        return _dot(h, w_ref[:, c0:c0 + PROJ_COL_GROUP])

    nxt = product(starts[0])
    for i, c0 in enumerate(starts):
        acc = nxt
        if i + 1 < len(starts):
            nxt = product(starts[i + 1])
        epilogue(acc, c0, refs)


def _proj(h, w, tn, epilogue, extra, extra_specs, out_shapes, out_specs, tm, name):
    m, d = h.shape
    ncols = w.shape[1]
    return pl.pallas_call(
        functools.partial(_proj_kernel, epilogue),
        grid=(m // tm, ncols // tn),
        in_specs=[
            pl.BlockSpec((tm, d), lambda i, j: (i, 0)),
            pl.BlockSpec((d, tn), lambda i, j: (0, j)),
        ] + extra_specs,
        out_specs=out_specs,
        out_shape=out_shapes,
        compiler_params=_params("arbitrary", "arbitrary"),
        name=name,
    )(h, w, *extra)


PROJ_TOKENS = 1024
PROJ_COL_GROUP = 256


def _in_projection(h, ws, batch, seq, rope_tabs, q_norm, k_norm, lb_f, lb_b, tag):
    m = h.shape[0]
    tm = PROJ_TOKENS
    rope = rope_tabs is not None
    if seq >= tm:
        tpb = seq // tm
        head_block = (1, tm)

        def head_index(i, j):
            return (i // tpb, 0, i % tpb, 0)
    else:
        tpb = 1
        head_block = (tm // seq, seq)

        def head_index(i, j):
            return (i, 0, 0, 0)

    def head_spec(nh):
        return pl.BlockSpec((head_block[0], nh, head_block[1], HEAD_DIM), head_index)

    def head_shape(nh, dt=BF):
        return jax.ShapeDtypeStruct((batch, nh, seq, HEAD_DIM), dt)

    vec_spec = pl.BlockSpec((1, HEAD_DIM), lambda i, j: (0, 0))
    tab_spec = pl.BlockSpec((tm, HEAD_DIM), lambda i, j: (i % tpb, 0))
    rope_in = list(rope_tabs) if rope else []
    rope_specs = [tab_spec, tab_spec] if rope else []

    q = _proj(h, ws["q"], ATT_WIDTH, functools.partial(_q_epilogue, rope),
              [q_norm] + rope_in, [vec_spec] + rope_specs,
              head_shape(ATT_HEADS), head_spec(ATT_HEADS), tm, "proj_q_" + tag)

    kv_shapes = [head_shape(ATT_KV_HEADS), head_shape(ATT_KV_HEADS)]
    kv_specs = [head_spec(ATT_KV_HEADS), head_spec(ATT_KV_HEADS)]
    if not rope:
        tok_spec = pl.BlockSpec((tm, KV_WIDTH), lambda i, j: (i, 0))
        kv_shapes += [jax.ShapeDtypeStruct((m, KV_WIDTH), F32)] * 2
        kv_specs += [tok_spec, tok_spec]
    kv = _proj(h, ws["kv"], 2 * KV_WIDTH, functools.partial(_kv_epilogue, rope),
               [k_norm] + rope_in, [vec_spec] + rope_specs,
               kv_shapes, kv_specs, tm, "proj_kv_" + tag)

    def heads(seg, act):
        return _proj(h, ws[seg], HG_WIDTH, functools.partial(_heads_epilogue, act),
                     [], [], head_shape(HG_HEADS), head_spec(HG_HEADS), tm, "proj_" + seg + "_" + tag)

    qh = heads("qh", "silu")
    vh = heads("vh", "none")
    og = heads("og", "silu")

    def forget(seg, lbp):
        return _proj(h, ws[seg], HG_WIDTH, _forget_epilogue,
                     [lbp], [pl.BlockSpec(lbp.shape, lambda i, j: (0, 0))],
                     [head_shape(HG_HEADS, F32), head_shape(HG_HEADS)],
                     [head_spec(HG_HEADS), head_spec(HG_HEADS)], tm, "proj_" + seg + "_" + tag)

    lf_f, k_f = forget("zf", lb_f)
    lf_b, k_b = forget("zb", lb_b)

    tn = 1024
    gates = _proj(h, ws["gates"], tn, _sigmoid_epilogue, [], [],
                  jax.ShapeDtypeStruct((m, 2 * D_MODEL), BF),
                  pl.BlockSpec((tm, tn), lambda i, j: (i, j)), tm, "proj_gates_" + tag)
    return q, kv, qh, vh, og, lf_f, k_f, lf_b, k_b, gates


def _attn_kernel(q_ref, k_ref, v_ref, o_ref, vt_scr):
    @pl.when(pl.program_id(2) == 0)
    def _():
        vt_scr[0:HEAD_DIM, :] = v_ref[0, 0].T
        vt_scr[HEAD_DIM:, :] = jnp.ones((ATTN_ONES_ROWS, vt_scr.shape[1]), BF)

    tq = q_ref.shape[2]
    keys = k_ref.shape[2]
    n_chunks = keys // ATTN_KEY_CHUNK
    n_pass = GQA_GROUP // ATTN_HEADS_PER_PASS
    qs = [jnp.concatenate([q_ref[0, g] for g in range(ps * ATTN_HEADS_PER_PASS,
                                                       (ps + 1) * ATTN_HEADS_PER_PASS)], axis=0)
          for ps in range(n_pass)]
    items = [(c, ps) for c in range(n_chunks) for ps in range(n_pass)]

    def scores(item):
        c, ps = item
        return _dot_nt(k_ref[0, 0, c * ATTN_KEY_CHUNK:(c + 1) * ATTN_KEY_CHUNK, :], qs[ps])

    state = [None] * n_pass
    pending = [scores(it) for it in items[:ATTN_LOOKAHEAD]]
    for i, (c, ps) in enumerate(items):
        ks = slice(c * ATTN_KEY_CHUNK, (c + 1) * ATTN_KEY_CHUNK)
        s = pending.pop(0)
        if i + ATTN_LOOKAHEAD < len(items):
            pending.append(scores(items[i + ATTN_LOOKAHEAD]))
        smax = jnp.max(s, axis=0, keepdims=True)
        if c == 0:
            state[ps] = (smax, _dot(vt_scr[:, ks], jnp.exp2(s - smax).astype(BF)))
        else:
            m, acc = state[ps]
            m_new = jnp.maximum(m, smax)
            p = jnp.exp2(s - m_new).astype(BF)
            state[ps] = (m_new, acc * jnp.exp2(m - m_new) + _dot(vt_scr[:, ks], p))
    for ps in range(n_pass):
        acc = state[ps][1]
        o = acc[0:HEAD_DIM] / acc[HEAD_DIM:HEAD_DIM + 1]
        for i in range(ATTN_HEADS_PER_PASS):
            g = ps * ATTN_HEADS_PER_PASS + i
            o_ref[0, :, g * HEAD_DIM:(g + 1) * HEAD_DIM] = o[:, i * tq:(i + 1) * tq].T.astype(BF)


def _attention(q, k, v, tq, name):
    batch, _, seq, _ = q.shape
    keys = k.shape[2]
    kv_spec = pl.BlockSpec((1, 1, keys, HEAD_DIM), lambda b, kv, t: (b, kv, 0, 0))
    return pl.pallas_call(
        _attn_kernel,
        grid=(batch, ATT_KV_HEADS, seq // tq),
        in_specs=[pl.BlockSpec((1, GQA_GROUP, tq, HEAD_DIM), lambda b, kv, t: (b, kv, t, 0)),
                  kv_spec, kv_spec],
        out_specs=pl.BlockSpec((1, tq, GQA_GROUP * HEAD_DIM), lambda b, kv, t: (b, t, kv)),
        out_shape=jax.ShapeDtypeStruct((batch, seq, ATT_WIDTH), BF),
        scratch_shapes=[pltpu.VMEM((HEAD_DIM + ATTN_ONES_ROWS, keys), BF)],
        compiler_params=_params("arbitrary", "arbitrary", "arbitrary"),
        name=name,
    )(q, k, v)


def _split3(x):
    hi = x.astype(BF)
    r1 = x - hi.astype(F32)
    mid = r1.astype(BF)
    lo = (r1 - mid.astype(F32)).astype(BF)
    return hi, mid, lo


def _block_reference(b3, hb, rev):
    n8 = b3.shape[0]
    if hb < 8:
        def row(r):
            return jnp.broadcast_to(b3[:, r:r + 1, :], b3.shape)
        if hb == 4:
            return row(4 if rev else 3)
        assert hb == 2
        sub = lax.broadcasted_iota(jnp.int32, b3.shape, 1)
        r0, r1 = (2, 6) if rev else (1, 5)
        return jnp.where(sub < 4, row(r0), row(r1))
    g = hb // 8
    nblk = n8 // (2 * g)
    edge = (b3[:, 0:1, :] if rev else b3[:, 7:8, :]).reshape(nblk, 2 * g, 1, b3.shape[2])
    pick = edge[:, g:g + 1] if rev else edge[:, g - 1:g]
    return jnp.broadcast_to(pick, (nblk, 2 * g, 8, b3.shape[2])).reshape(b3.shape)


def _chunk_cumsum(lf, tri_b):
    hi, mid, lo = _split3(lf)
    return _dot(tri_b, hi) + _dot(tri_b, mid) + _dot(tri_b, lo)


def _rows8(x, r):
    return x[r * 8:(r + 1) * 8]


def _hg_products(q, k, lf, b, vt, rev):
    c = q.shape[0]
    n8 = c // 8
    b3 = b.reshape(n8, 8, HEAD_DIM)
    q32 = q.astype(F32)
    k32 = k.astype(F32)
    sub = lax.broadcasted_iota(jnp.int32, (c, HEAD_DIM), 0)
    qbit = 0 if rev else 1
    products = [(HG_DIAG_LEVEL, list(range(n8)), _dot_nt(q, k))]
    hb, j = 1, 0
    while hb < c:
        if hb < 8:
            is_query = ((sub & hb) == 0) if rev else ((sub & hb) != 0)
            if hb == 1:
                arg = jnp.where(is_query, lf, 0.0)
            else:
                arg = -jnp.abs(b - _block_reference(b3, hb, rev).reshape(c, HEAD_DIM))
            x = (jnp.where(is_query, q32, k32) * jnp.exp2(arg)).astype(BF)
            prod = _dot_nt(x, x)
            q_groups = list(range(n8))
        else:
            q_groups = [r for r in range(n8) if ((r * 8 // hb) & 1) == qbit]
            ref = _block_reference(b3, hb, rev).reshape(c, HEAD_DIM)
            mixed = jnp.concatenate([_rows8(q32 if r in q_groups else k32, r) for r in range(n8)], axis=0)
            arg = jnp.concatenate([_rows8(b, r) - _rows8(ref, r) if r in q_groups
                                   else _rows8(ref, r) - _rows8(b, r) for r in range(n8)], axis=0)
            y = mixed * jnp.exp2(arg)
            lhs = jnp.concatenate([_rows8(y, r) for r in q_groups], axis=0).astype(BF)
            prod = _dot_nt(lhs, y.astype(BF))
        products.append((j, q_groups, prod))
        hb, j = 2 * hb, j + 1
    btot = b[0:1, :] if rev else b[c - 1:c, :]
    qs = (q32 * jnp.exp2(b)).astype(BF)
    ks = (k32 * jnp.exp2(btot - b)).astype(BF)
    return products, qs, jnp.exp2(btot), _dot(vt, ks)


def _hg_finish(parts, vt, st, level):
    products, qs, decay, contribution = parts
    a_rows = [None] * (level.shape[0] // 8)
    for code, q_groups, prod in products:
        for i, r in enumerate(q_groups):
            keep = _rows8(level, r) == code
            a_rows[r] = jnp.where(keep, _rows8(prod, i), 0.0 if a_rows[r] is None else a_rows[r])
    a = jnp.concatenate(a_rows, axis=0)
    o = _dot_nt(jnp.concatenate([a.astype(BF), qs], axis=1),
                jnp.concatenate([vt, st.astype(BF)], axis=1))
    return o, st * decay + contribution


def _pair_levels(c, rev):
    t = lax.broadcasted_iota(jnp.int32, (c, c), 0)
    s = lax.broadcasted_iota(jnp.int32, (c, c), 1)
    x = t ^ s
    lvl = jnp.zeros((c, c), jnp.int32)
    p = 2
    while p < c:
        lvl = lvl + jnp.where(x >= p, 1, 0)
        p *= 2
    valid = (s > t) if rev else (s < t)
    return jnp.where(valid, lvl, jnp.where(s == t, HG_DIAG_LEVEL, -1))


def _hgrn_kernel(seq, heads, zero_init, q_ref, kf_ref, kb_ref, lf_ref, lb_ref, v_ref, og_ref, *refs):
    if zero_init:
        hn_ref, o_ref, sf_ref, sb_ref, vt_scr, bf_scr, bb_scr, of_scr, ob_scr = refs
    else:
        s0f_ref, s0b_ref, hn_ref, o_ref, vt_scr, bf_scr, bb_scr, of_scr, ob_scr = refs
    c = HG_CHUNK
    n = seq // c
    t = lax.broadcasted_iota(jnp.int32, (c, c), 0)
    s = lax.broadcasted_iota(jnp.int32, (c, c), 1)
    tril_b = jnp.where(s <= t, 1.0, 0.0).astype(BF)
    triu_b = jnp.where(s >= t, 1.0, 0.0).astype(BF)
    level_f = _pair_levels(c, False)
    level_b = _pair_levels(c, True)

    def rows(i):
        return pl.ds(pl.multiple_of(i * c, c), c)

    hpt = vt_scr.shape[0]

    def head_body(p, _):
        group = min(n, HG_GROUP)
        slots = [(sl, p * hpt + sl) for sl in range(hpt)]

        def prepare(g, _):
            done = []
            for sl, h in slots:
                for u in range(group):
                    i = g * group + u
                    r = rows(i)
                    done.append((sl, i, r, v_ref[0, h, r, :].T,
                                 _chunk_cumsum(lf_ref[0, h, r, :], tril_b),
                                 _chunk_cumsum(lb_ref[0, h, r, :], triu_b)))
            for sl, i, r, vt, bf, bb in done:
                vt_scr[sl, i] = vt
                bf_scr[sl, r, :] = bf
                bb_scr[sl, r, :] = bb
            return 0

        lax.fori_loop(0, n // group, prepare, 0)

        def body(g, carry):
            states = list(carry)
            work = []
            for u in range(group):
                i = g * group + u
                jf, jb = i, n - 1 - i
                rf, rb = rows(jf), rows(jb)
                for sl, h in slots:
                    work.append((sl, rf, jf,
                                 _hg_products(q_ref[0, h, rf, :], kf_ref[0, h, rf, :], lf_ref[0, h, rf, :],
                                              bf_scr[sl, rf, :], vt_scr[sl, jf], False),
                                 rb, jb,
                                 _hg_products(q_ref[0, h, rb, :], kb_ref[0, h, rb, :], lb_ref[0, h, rb, :],
                                              bb_scr[sl, rb, :], vt_scr[sl, jb], True)))
            for sl, rf, jf, parts_f, rb, jb, parts_b in work:
                o_f, states[2 * sl] = _hg_finish(parts_f, vt_scr[sl, jf], states[2 * sl], level_f)
                of_scr[sl, rf, :] = o_f
                o_b, states[2 * sl + 1] = _hg_finish(parts_b, vt_scr[sl, jb], states[2 * sl + 1], level_b)
                ob_scr[sl, rb, :] = o_b
            return tuple(states)

        st0 = []
        for sl, h in slots:
            if zero_init:
                st0 += [jnp.zeros((HEAD_DIM, HEAD_DIM), F32), jnp.zeros((HEAD_DIM, HEAD_DIM), F32)]
            else:
                st0 += [s0f_ref[0, h].T, s0b_ref[0, h].T]
        states = lax.fori_loop(0, n // group, body, tuple(st0), unroll=min(n // group, HG_UNROLL))
        for sl, h in slots:
            o = of_scr[sl] + ob_scr[sl]
            y = _rms(o) * hn_ref[...] * og_ref[0, h].astype(F32)
            o_ref[0, h] = y.astype(BF)
            if zero_init:
                sf_ref[0, h] = states[2 * sl].T
                sb_ref[0, h] = states[2 * sl + 1].T
        return 0

    lax.fori_loop(0, heads // hpt, head_body, 0)


def _hgrn(qh, k_f, k_b, lf_f, lf_b, vh, og, hg_norm, s0f, s0b, heads_per_step, name):
    batch, heads, seq, hd = qh.shape
    hps = heads_per_step
    hpt = min(hps, max(1, HG_GROUP // (seq // HG_CHUNK)), HG_MAX_HEADS_PER_TRIP)
    zero_init = s0f is None
    head_spec = pl.BlockSpec((1, hps, seq, hd), lambda b, h: (b, h, 0, 0))
    state_spec = pl.BlockSpec((1, hps, hd, hd), lambda b, h: (b, h, 0, 0))
    vec_spec = pl.BlockSpec((1, hd), lambda b, h: (0, 0))
    o_shape = jax.ShapeDtypeStruct((batch, heads, seq, hd), BF)
    ins = [qh, k_f, k_b, lf_f, lf_b, vh, og]
    in_specs = [head_spec] * 7
    if zero_init:
        state_shape = jax.ShapeDtypeStruct((batch, heads, hd, hd), F32)
        out_shape = [o_shape, state_shape, state_shape]
        out_specs = [head_spec, state_spec, state_spec]
    else:
        ins += [s0f, s0b]
        in_specs += [state_spec, state_spec]
        out_shape = o_shape
        out_specs = head_spec
    ins.append(hg_norm)
    in_specs.append(vec_spec)
    return pl.pallas_call(
        functools.partial(_hgrn_kernel, seq, hps, zero_init),
        grid=(batch, heads // hps),
        in_specs=in_specs,
        out_specs=out_specs,
        out_shape=out_shape,
        scratch_shapes=[pltpu.VMEM((hpt, seq // HG_CHUNK, hd, HG_CHUNK), BF)]
        + [pltpu.VMEM((hpt, seq, hd), F32)] * 4,
        compiler_params=_params("arbitrary", "arbitrary"),
        name=name,
    )(*ins)


def _merge_kernel(att_ref, hg_ref, ga_ref, gb_ref, x_ref, wa_ref, wh_ref, wo_ref,
                  g1_ref, sc2_ref, sh2_ref, npost_ref, npre_ref, x1_ref, h2_ref):
    a = _dot(att_ref[...], wa_ref[...])
    hg = jnp.concatenate([hg_ref[0, h] for h in range(HG_HEADS)], axis=1)
    b = _dot(hg, wh_ref[...])
    m = (ga_ref[...].astype(F32) * a + gb_ref[...].astype(F32) * b).astype(BF)
    mo = _dot(m, wo_ref[...])
    x1 = x_ref[...] + g1_ref[0] * (_rms(mo) * npost_ref[...])
    x1_ref[...] = x1
    h2 = _rms(x1) * npre_ref[...]
    h2_ref[...] = (h2 * (1.0 + sc2_ref[0]) + sh2_ref[0]).astype(BF)


def _merge(att, hg, gates, x, wa, wh, wo, g1, sc2, sh2, npost, npre, seq, name):
    m, d = x.shape
    tm = 256
    tpb = seq // tm
    w_att = att.shape[1]
    hg_tpb = hg.shape[2] // tm
    hg_spec = pl.BlockSpec((1, HG_HEADS, tm, HEAD_DIM), lambda i: (i // hg_tpb, 0, i % hg_tpb, 0))

    def const(shape):
        return pl.BlockSpec(shape, lambda i: (0, 0), pipeline_mode=pl.Buffered(1))

    mod_spec = pl.BlockSpec((1, 1, d), lambda i: (i // tpb, 0, 0))
    vec_spec = pl.BlockSpec((1, d), lambda i: (0, 0))
    return pl.pallas_call(
        _merge_kernel,
        grid=(m // tm,),
        in_specs=[
            pl.BlockSpec((tm, w_att), lambda i: (i, 0)),
            hg_spec,
            pl.BlockSpec((tm, d), lambda i: (i, 0)),
            pl.BlockSpec((tm, d), lambda i: (i, 1)),
            pl.BlockSpec((tm, d), lambda i: (i, 0)),
            const(wa.shape), const(wh.shape), const(wo.shape),
            mod_spec, mod_spec, mod_spec, vec_spec, vec_spec,
        ],
        out_specs=[pl.BlockSpec((tm, d), lambda i: (i, 0)), pl.BlockSpec((tm, d), lambda i: (i, 0))],
        out_shape=[jax.ShapeDtypeStruct((m, d), F32), jax.ShapeDtypeStruct((m, d), BF)],
        compiler_params=_params("arbitrary"),
        name=name,
    )(att, hg, gates, gates, x, wa, wh, wo, g1, sc2, sh2, npost, npre)


def _ffn_up_kernel(h_ref, wg_ref, wu_ref, a_ref):
    h = h_ref[...]
    g = _dot(h, wg_ref[...])
    u = _dot(h, wu_ref[...])
    a_ref[...] = (g * jax.nn.sigmoid(g) * u).astype(BF)


def _ffn_down_kernel(a_ref, wo_ref, x1_ref, g2_ref, npost_ref, o_ref):
    y = _rms(_dot(a_ref[...], wo_ref[...])) * npost_ref[...]
    o_ref[...] = x1_ref[...] + g2_ref[0] * y


def _ffn(h2, x1, w_in, w_out, g2, npost, seq, name):
    m, d = x1.shape
    tm, tf = PROJ_TOKENS, 512
    nf = D_FF // tf
    act = pl.pallas_call(
        _ffn_up_kernel,
        grid=(m // tm, nf),
        in_specs=[
            pl.BlockSpec((tm, d), lambda i, f: (i, 0)),
            pl.BlockSpec((d, tf), lambda i, f: (0, f)),
            pl.BlockSpec((d, tf), lambda i, f: (0, nf + f)),
        ],
        out_specs=pl.BlockSpec((tm, tf), lambda i, f: (i, f)),
        out_shape=jax.ShapeDtypeStruct((m, D_FF), BF),
        compiler_params=_params("arbitrary", "arbitrary"),
        name=name + "_up",
    )(h2, w_in, w_in)
    td = 256
    tpb = seq // td
    return pl.pallas_call(
        _ffn_down_kernel,
        grid=(m // td,),
        in_specs=[
            pl.BlockSpec((td, D_FF), lambda i: (i, 0)),
            pl.BlockSpec((D_FF, d), lambda i: (0, 0), pipeline_mode=pl.Buffered(1)),
            pl.BlockSpec((td, d), lambda i: (i, 0)),
            pl.BlockSpec((1, 1, d), lambda i: (i // tpb, 0, 0)),
            pl.BlockSpec((1, d), lambda i: (0, 0)),
        ],
        out_specs=pl.BlockSpec((td, d), lambda i: (i, 0)),
        out_shape=jax.ShapeDtypeStruct((m, d), F32),
        compiler_params=_params("arbitrary"),
        name=name + "_down",
    )(act, w_out, x1, g2, npost)


def _rope_tables(n_tokens):
    rows = n_tokens // GRID_W
    half = HEAD_DIM // 4
    r = jnp.repeat(jnp.arange(rows), GRID_W).astype(F32)
    col = jnp.tile(jnp.arange(GRID_W), rows).astype(F32)
    inv = ROPE_THETA ** (-jnp.arange(half, dtype=F32) / half)
    ar = r[:, None] * inv
    ac = col[:, None] * inv
    cos = jnp.concatenate([jnp.cos(ar), jnp.cos(ar), jnp.cos(ac), jnp.cos(ac)], axis=-1)
    sin = jnp.concatenate([-jnp.sin(ar), jnp.sin(ar), -jnp.sin(ac), jnp.sin(ac)], axis=-1)
    return cos, sin


def _layer(x3, mods, weights, rope_tabs, cache, states, tag):
    batch, seq, d = x3.shape
    m = batch * seq
    x = x3.reshape(m, d)
    (w_in, q_norm, k_norm, lb_f, lb_b, hg_norm, wa, wh, wo, w_ffn_in, w_ffn_out,
     n_pre_mix, n_post_mix, n_pre_ffn, n_post_ffn) = weights
    nb = mods.shape[0]
    sh1, sc1, g1, sh2, sc2, g2 = [mods[:, i].reshape(nb, 1, d) for i in range(6)]
    seq_mod = seq if nb == batch else m

    h = _modnorm(x, n_pre_mix, sc1, sh1, seq_mod)
    q, kv, qh, vh, og, lf_f, k_f, lf_b, k_b, gates = _in_projection(
        h, w_in, batch, seq, rope_tabs, q_norm, k_norm, lb_f, lb_b, tag)

    if cache is None:
        k_bf, v_bf, k_f32, v_f32 = kv
        att = _attention(q, k_bf, v_bf, seq, "attn_" + tag)
    else:
        k_bf, v_bf = kv
        att = _attention(q, jnp.concatenate([k_bf, cache[0]], axis=2),
                         jnp.concatenate([v_bf, cache[1]], axis=2), 256, "attn_" + tag)
        k_f32 = v_f32 = None

    hps = HG_HEADS if seq <= 2 * HG_CHUNK else 1
    if states is None:
        hg, s_f, s_b = _hgrn(qh, k_f, k_b, lf_f, lf_b, vh, og, hg_norm, None, None, hps, "hgrn_" + tag)
    else:
        hg = _hgrn(qh, k_f, k_b, lf_f, lf_b, vh, og, hg_norm, states[0], states[1], hps, "hgrn_" + tag)
        s_f = s_b = None

    x1, h2 = _merge(att.reshape(m, ATT_WIDTH), hg, gates, x, wa, wh, wo,
                    g1, sc2, sh2, n_post_mix, n_pre_ffn, seq_mod, "merge_" + tag)
    y = _ffn(h2, x1, w_ffn_in, w_ffn_out, g2, n_post_ffn, seq_mod, "ffn_" + tag)
    return y.reshape(batch, seq, d), k_f32, v_f32, s_f, s_b


def kernel(x_prompt, x_sample, cache_k, cache_v, state_fwd, state_bwd, c, c_ctx, w_ada, b_ada,
           norm_pre_mix, norm_post_mix, norm_pre_ffn, norm_post_ffn, w_in, q_norm, k_norm,
           lb_fwd, lb_bwd, hg_norm, w_br_att, w_br_hg, w_out, w_ffn_in, w_ffn_out):
    depth = w_in.shape[0]
    assert depth == 1 and lb_fwd.shape[0] == 2
    batch, seq, d = x_prompt.shape
    dec_batch, dec_seq, _ = x_sample.shape
    past = cache_k.shape[2]

    cond = jnp.concatenate([c_ctx[None, :], c, jnp.zeros((8 - 1 - dec_batch, d), F32)], axis=0)
    mods = _ada(cond, w_ada[0], b_ada[0][None, :]).reshape(8, 6, d)

    segments = (("q", OFF_Q, OFF_KV), ("kv", OFF_KV, OFF_QH), ("qh", OFF_QH, OFF_ZF),
                ("zf", OFF_ZF, OFF_ZB), ("zb", OFF_ZB, OFF_VH), ("vh", OFF_VH, OFF_OG),
                ("og", OFF_OG, OFF_GA), ("gates", OFF_GA, OFF_GB + D_MODEL))
    w_in_segments = {name: w_in[0, :, a:b].astype(BF) for name, a, b in segments}
    weights = (
        w_in_segments, q_norm, k_norm, lb_fwd, lb_bwd, hg_norm,
        w_br_att[0].astype(BF), w_br_hg[0].astype(BF), w_out[0].astype(BF),
        w_ffn_in[0].astype(BF), w_ffn_out[0].astype(BF),
        norm_pre_mix, norm_post_mix, norm_pre_ffn, norm_post_ffn,
    )

    y_p, k_c, v_c, s_f, s_b = _layer(x_prompt, mods[0:1], weights, None, None, None, "ctx")

    cache = (
        cache_k[:, 0].transpose(0, 2, 1, 3).astype(BF),
        cache_v[:, 0].transpose(0, 2, 1, 3).astype(BF),
    )
    states = (state_fwd[:, 0], state_bwd[:, 0])
    y_s, _, _, _, _ = _layer(x_sample, mods[1:1 + dec_batch], weights, _rope_tables(dec_seq),
                             cache, states, "lat")

    new_k = k_c.reshape(batch, 1, seq, ATT_KV_HEADS, HEAD_DIM)
    new_v = v_c.reshape(batch, 1, seq, ATT_KV_HEADS, HEAD_DIM)
    return (y_p, y_s, new_k, new_v, s_f[:, None], s_b[:, None])
```
